```python
import jax, jax.numpy as jnp
from jax import lax
import numpy as np

D_MODEL = 1024
BATCH = 4
SEQ = 4096
DEPTH = 4

GRID_W = 64
CTX_LEN = 256
N_MIXERS = 3
RW_HEAD = 64
RW_HEADS = D_MODEL // RW_HEAD
RW_DECAY_LORA = 64
RW_AAA_LORA = 64
RW_GATE_LORA = 128
AT_HEAD = 128
AT_Q_HEADS = D_MODEL // AT_HEAD
AT_KV_HEADS = 2
Q_BLOCK = 128
ROPE_BASE = 10000.0
NA_HEAD = 64
NA_HEADS = D_MODEL // NA_HEAD
WIN_H = 8
WIN_W = 16
D_FF = -(-8 * D_MODEL // (3 * 256)) * 256
N_RWKV = len(range(0, DEPTH, N_MIXERS))
N_GQA = len(range(1, DEPTH, N_MIXERS))
N_NA = len(range(2, DEPTH, N_MIXERS))
RMS_EPS = 1e-6
LNX_EPS = 64e-5
NORM_EPS = 1e-12

kernel_name = "hybrid_rwkv7_gqa_natten_prefix_dit"


def rms_norm(x, g):
    xf = x.astype(jnp.float32)
    y = xf * lax.rsqrt(jnp.mean(xf * xf, axis=-1, keepdims=True) + RMS_EPS)
    return (y * g).astype(x.dtype)


def swiglu(h, w1, w3, w2):
    return (jax.nn.silu(h @ w1) * (h @ w3)) @ w2


def softmax_attend(q, k, v):
    s = jnp.einsum('bkgqd,bksd->bkgqs', q, k).astype(jnp.float32) * (q.shape[-1] ** -0.5)
    p = jax.nn.softmax(s, axis=-1).astype(v.dtype)
    return jnp.einsum('bkgqs,bksd->bkgqd', p, v)


def axial_rope(n_tok, head_dim):
    t = jnp.arange(n_tok)
    rows = (t // GRID_W).astype(jnp.float32)
    cols = (t % GRID_W).astype(jnp.float32)
    d_axis = head_dim // 2
    inv = jnp.float32(ROPE_BASE) ** (-jnp.arange(0, d_axis, 2, dtype=jnp.float32) / d_axis)
    ang = jnp.concatenate([rows[:, None] * inv, cols[:, None] * inv], axis=-1)
    return jnp.cos(ang), jnp.sin(ang)


def apply_rope(x, cos, sin):
    xp = x.reshape(x.shape[:-1] + (-1, 2))
    x1, x2 = xp[..., 0], xp[..., 1]
    c, s = cos[None, :, None, :], sin[None, :, None, :]
    out = jnp.stack([x1 * c - x2 * s, x1 * s + x2 * c], axis=-1)
    return out.reshape(x.shape).astype(x.dtype)


def centred_shift(h):
    prev = jnp.pad(h[:, :-1], ((0, 0), (1, 0), (0, 0)))
    nxt = jnp.pad(h[:, 1:], ((0, 0), (0, 1), (0, 0)))
    return 0.5 * (prev + nxt) - h


def _heads(t):
    return t.reshape(t.shape[:-1] + (RW_HEADS, RW_HEAD))


def rwkv7_features(h, mu, wr, wk, wv, w0, w1, w2, a0, a1, a2, k_k, k_a):
    f32 = jnp.float32
    xx = centred_shift(h)
    xr, xw, xk, xv, xa = (h + xx * mu[i] for i in range(5))
    r = _heads(xr @ wr)
    k = _heads(xk @ wk)
    v = _heads(xv @ wv)
    kk = (k * _heads(k_k)).astype(f32)
    kk = kk / jnp.maximum(jnp.linalg.norm(kk, axis=-1, keepdims=True), NORM_EPS)
    lora_w = jnp.einsum('ebtr,erd->ebtd', jnp.tanh(jnp.einsum('btd,edr->ebtr', xw, w1)), w2)
    w_log = -jax.nn.softplus(-(w0[:, None, None, :] + lora_w).astype(f32)) - 0.5
    decay = _heads(jnp.exp(-jnp.exp(w_log)))
    lora_a = jnp.einsum('ebtr,erd->ebtd', jnp.einsum('btd,edr->ebtr', xa, a1), a2)
    a = _heads(jax.nn.sigmoid((a0[:, None, None, :] + lora_a).astype(f32)))
    k_dir = k[None] * (1.0 + (a - 1.0) * _heads(k_a))
    return xx, r, v, kk, decay, a, k_dir


def rwkv7_scan(S0, r, decay, k, v, kk, a, reverse):
    def step(S, inp):
        r_t, w_t, k_t, v_t, kk_t, a_t = inp
        sa = jnp.einsum('bhvk,bhk->bhv', S, -kk_t)
        S = (S * w_t[:, :, None, :] + sa[..., None] * (kk_t * a_t)[:, :, None, :]
             + v_t[..., None] * k_t[:, :, None, :])
        return S, jnp.einsum('bhvk,bhk->bhv', S, r_t)
    xs = tuple(jnp.moveaxis(t.astype(jnp.float32), 1, 0) for t in (r, decay, k, v, kk, a))
    S, ys = lax.scan(step, S0, xs, reverse=reverse)
    return S, jnp.moveaxis(ys, 0, 1)


def rwkv7_mixer(h_lat, h_ctx, mu, wr, wk, wv, wo, w0, w1, w2, a0, a1, a2, g1, g2,
                k_k, k_a, r_k, lnx_g, lnx_b, need_ctx):
    feat_l = rwkv7_features(h_lat, mu, wr, wk, wv, w0, w1, w2, a0, a1, a2, k_k, k_a)
    feat_c = rwkv7_features(h_ctx, mu, wr, wk, wv, w0, w1, w2, a0, a1, a2, k_k, k_a)
    xx_l, r_l, v_l, kk_l, d_l, a_l, k_l = feat_l
    xx_c, r_c, v_c, kk_c, d_c, a_c, k_c = feat_c
    S0 = jnp.zeros((h_lat.shape[0], RW_HEADS, RW_HEAD, RW_HEAD), jnp.float32)
    ys_l, ys_c = [], []
    for d, rev in enumerate((False, True)):
        S_c, yc_d = rwkv7_scan(S0, r_c, d_c[d], k_c[d], v_c, kk_c, a_c[d], rev)
        _, yl_d = rwkv7_scan(S_c, r_l, d_l[d], k_l[d], v_l, kk_l, a_l[d], rev)
        ys_l.append(yl_d)
        ys_c.append(yc_d)

    def readout(h, xx, y, r, k_dir, v):
        B, T, D = h.shape
        mean = jnp.mean(y, axis=-1, keepdims=True)
        var = jnp.mean(jnp.square(y - mean), axis=-1, keepdims=True)
        yn = ((y - mean) * lax.rsqrt(var + LNX_EPS)).reshape(B, T, D) * lnx_g + lnx_b
        bonus = jnp.sum(jnp.sum(r[None] * k_dir * r_k, axis=-1, keepdims=True) * v[None], axis=0)
        g = jax.nn.sigmoid((h + xx * mu[5]) @ g1) @ g2
        return ((yn + bonus.reshape(B, T, D)) * g).astype(h.dtype) @ wo

    o_lat = readout(h_lat, xx_l, ys_l[0] + ys_l[1], r_l, k_l, v_l)
    o_ctx = readout(h_ctx, xx_c, ys_c[0] + ys_c[1], r_c, k_c, v_c) if need_ctx else None
    return o_lat, o_ctx


def gqa_mixer(h_lat, h_ctx, wq, wk, wv, wo, gq, gk, need_ctx):
    B, T, _ = h_lat.shape
    G = AT_Q_HEADS // AT_KV_HEADS

    def proj(h):
        n = h.shape[1]
        q = rms_norm((h @ wq).reshape(B, n, AT_Q_HEADS, AT_HEAD), gq)
        k = rms_norm((h @ wk).reshape(B, n, AT_KV_HEADS, AT_HEAD), gk)
        v = (h @ wv).reshape(B, n, AT_KV_HEADS, AT_HEAD)
        return q, k, v

    def to_groups(q):
        return q.reshape(B, q.shape[1], AT_KV_HEADS, G, AT_HEAD).transpose(0, 2, 3, 1, 4)

    def from_groups(o):
        return o.transpose(0, 3, 1, 2, 4).reshape(B, o.shape[3], AT_Q_HEADS * AT_HEAD)

    q_l, k_l, v_l = proj(h_lat)
    q_c, k_c, v_c = proj(h_ctx)
    cos, sin = axial_rope(T, AT_HEAD)
    q_l = apply_rope(q_l, cos, sin)
    k_l = apply_rope(k_l, cos, sin)
    k_all = jnp.concatenate([k_l, k_c], axis=1).transpose(0, 2, 1, 3)
    v_all = jnp.concatenate([v_l, v_c], axis=1).transpose(0, 2, 1, 3)
    nb = T // Q_BLOCK
    qb = to_groups(q_l).reshape(B, AT_KV_HEADS, G, nb, Q_BLOCK, AT_HEAD).transpose(3, 0, 1, 2, 4, 5)
    ob = lax.map(lambda q: softmax_attend(q, k_all, v_all), qb)
    o_l = ob.transpose(1, 2, 3, 0, 4, 5).reshape(B, AT_KV_HEADS, G, T, AT_HEAD)
    o_lat = from_groups(o_l) @ wo
    o_ctx = None
    if need_ctx:
        o_c = softmax_attend(to_groups(q_c), k_c.transpose(0, 2, 1, 3), v_c.transpose(0, 2, 1, 3))
        o_ctx = from_groups(o_c) @ wo
    return o_lat, o_ctx


def na_mixer(h_lat, h_ctx, wqkv, wo, gq, gk, rpb, need_ctx):
    B, T, _ = h_lat.shape
    rows = T // GRID_W
    kh, kw = min(WIN_H, rows), WIN_W
    n_win = kh * kw

    def proj(h):
        qkv = (h @ wqkv).reshape(B, h.shape[1], 3, NA_HEADS, NA_HEAD)
        q = rms_norm(qkv[:, :, 0], gq).transpose(0, 2, 1, 3)
        k = rms_norm(qkv[:, :, 1], gk).transpose(0, 2, 1, 3)
        v = qkv[:, :, 2].transpose(0, 2, 1, 3)
        return q, k, v

    q_l, k_l, v_l = proj(h_lat)
    q_c, k_c, v_c = proj(h_ctx)
    r_idx = np.arange(rows)
    c_idx = np.arange(GRID_W)
    key_rows = np.clip(r_idx - kh // 2, 0, rows - kh)[:, None] + np.arange(kh)
    key_cols = np.clip(c_idx - kw // 2, 0, GRID_W - kw)[:, None] + np.arange(kw)
    idx = (key_rows[:, None, :, None] * GRID_W + key_cols[None, :, None, :]).reshape(rows, GRID_W, n_win)
    off_r = jnp.asarray(key_rows - r_idx[:, None] + (WIN_H - 1), jnp.int32)
    off_c = jnp.asarray(key_cols - c_idx[:, None] + (WIN_W - 1), jnp.int32)
    idx = jnp.asarray(idx, jnp.int32)
    scale = NA_HEAD ** -0.5
    q_rows = q_l.reshape(B, NA_HEADS, rows, GRID_W, NA_HEAD).transpose(2, 0, 1, 3, 4)

    def row_block(args):
        q_row, idx_row, offr_row = args
        k_win = k_l[:, :, idx_row]
        v_win = v_l[:, :, idx_row]
        bias = rpb[:, offr_row[None, :, None], off_c[:, None, :]].reshape(NA_HEADS, GRID_W, n_win)
        s_win = jnp.einsum('bhqd,bhqkd->bhqk', q_row, k_win).astype(jnp.float32) * scale + bias
        s_ctx = jnp.einsum('bhqd,bhcd->bhqc', q_row, k_c).astype(jnp.float32) * scale
        p = jax.nn.softmax(jnp.concatenate([s_win, s_ctx], axis=-1), axis=-1).astype(v_l.dtype)
        return (jnp.einsum('bhqk,bhqkd->bhqd', p[..., :n_win], v_win)
                + jnp.einsum('bhqc,bhcd->bhqd', p[..., n_win:], v_c))

    o = lax.map(row_block, (q_rows, idx, off_r))
    o_lat = o.transpose(1, 0, 3, 2, 4).reshape(B, T, NA_HEADS * NA_HEAD) @ wo
    o_ctx = None
    if need_ctx:
        o_c = softmax_attend(q_c[:, :, None], k_c, v_c)[:, :, 0]
        o_ctx = o_c.transpose(0, 2, 1, 3).reshape(B, h_ctx.shape[1], NA_HEADS * NA_HEAD) @ wo
    return o_lat, o_ctx


def setup_inputs(seed: int = 0) -> dict:
    key = jax.random.key(seed)
    keys = iter(jax.random.split(key, 64))
    f32 = jnp.float32
    D = D_MODEL
    nA, nB, nC = N_RWKV, N_GQA, N_NA
    A_DIM = AT_Q_HEADS * AT_HEAD
    KV_DIM = AT_KV_HEADS * AT_HEAD
    N_DIM = NA_HEADS * NA_HEAD

    def nrm(shape, std):
        return jax.random.normal(next(keys), shape, f32) * std

    def unif(shape, lo, hi):
        return jax.random.uniform(next(keys), shape, f32, lo, hi)

    return {
        "x": nrm((BATCH, SEQ, D), 1.0),
        "c": nrm((BATCH, D), 1.0),
        "ctx": nrm((BATCH, CTX_LEN, D), 1.0),
        "c_ctx": nrm((D,), 1.0),
        "mod_w": nrm((DEPTH, D, 6 * D), 0.5 * D ** -0.5),
        "mod_b": nrm((DEPTH, 6 * D), 0.02),
        "norm_mix": 1.0 + nrm((DEPTH, D), 0.05),
        "norm_ffn": 1.0 + nrm((DEPTH, D), 0.05),
        "ff_w1": nrm((DEPTH, D, D_FF), D ** -0.5),
        "ff_w3": nrm((DEPTH, D, D_FF), D ** -0.5),
        "ff_w2": nrm((DEPTH, D_FF, D), D_FF ** -0.5),
        "rw_mu": unif((nA, 6, D), 0.0, 1.0),
        "rw_wr": nrm((nA, D, D), D ** -0.5),
        "rw_wk": nrm((nA, D, D), D ** -0.5),
        "rw_wv": nrm((nA, D, D), D ** -0.5),
        "rw_wo": nrm((nA, D, D), D ** -0.5),
        "rw_w0": unif((nA, 2, D), -5.0, 0.5),
        "rw_w1": nrm((nA, 2, D, RW_DECAY_LORA), D ** -0.5),
        "rw_w2": nrm((nA, 2, RW_DECAY_LORA, D), 0.5 * RW_DECAY_LORA ** -0.5),
        "rw_a0": nrm((nA, 2, D), 0.1),
        "rw_a1": nrm((nA, 2, D, RW_AAA_LORA), D ** -0.5),
        "rw_a2": nrm((nA, 2, RW_AAA_LORA, D), 0.5 * RW_AAA_LORA ** -0.5),
        "rw_g1": nrm((nA, D, RW_GATE_LORA), D ** -0.5),
        "rw_g2": nrm((nA, RW_GATE_LORA, D), RW_GATE_LORA ** -0.5),
        "rw_kk": 0.85 + nrm((nA, D), 0.05),
        "rw_ka": 1.0 + nrm((nA, D), 0.05),
        "rw_rk": nrm((nA, RW_HEADS, RW_HEAD), 0.1),
        "rw_lnx_g": 1.0 + nrm((nA, D), 0.05),
        "rw_lnx_b": nrm((nA, D), 0.02),
        "at_wq": nrm((nB, D, A_DIM), D ** -0.5),
        "at_wk": nrm((nB, D, KV_DIM), D ** -0.5),
        "at_wv": nrm((nB, D, KV_DIM), D ** -0.5),
        "at_wo": nrm((nB, A_DIM, D), A_DIM ** -0.5),
        "at_gq": 1.0 + nrm((nB, AT_HEAD), 0.05),
        "at_gk": 1.0 + nrm((nB, AT_HEAD), 0.05),
        "na_wqkv": nrm((nC, D, 3 * N_DIM), D ** -0.5),
        "na_wo": nrm((nC, N_DIM, D), N_DIM ** -0.5),
        "na_gq": 1.0 + nrm((nC, NA_HEAD), 0.05),
        "na_gk": 1.0 + nrm((nC, NA_HEAD), 0.05),
        "na_rpb": nrm((nC, NA_HEADS, 2 * WIN_H - 1, 2 * WIN_W - 1), 0.1),
    }


def reference(x, c, ctx, c_ctx, mod_w, mod_b, norm_mix, norm_ffn, ff_w1, ff_w3, ff_w2,
              rw_mu, rw_wr, rw_wk, rw_wv, rw_wo, rw_w0, rw_w1, rw_w2, rw_a0, rw_a1, rw_a2,
              rw_g1, rw_g2, rw_kk, rw_ka, rw_rk, rw_lnx_g, rw_lnx_b,
              at_wq, at_wk, at_wv, at_wo, at_gq, at_gk,
              na_wqkv, na_wo, na_gq, na_gk, na_rpb):
    mod_lat = jnp.einsum('bd,lde->lbe', jax.nn.silu(c), mod_w) + mod_b[:, None, :]
    mod_ctx = jnp.einsum('d,lde->le', jax.nn.silu(c_ctx), mod_w) + mod_b
    h_lat, h_ctx = x, ctx
    for i in range(DEPTH):
        need_ctx = i < DEPTH - 1
        kind, j = i % N_MIXERS, i // N_MIXERS
        sh_m, sc_m, gt_m, sh_f, sc_f, gt_f = jnp.split(mod_lat[i][:, None, :], 6, axis=-1)
        csh_m, csc_m, cgt_m, csh_f, csc_f, cgt_f = jnp.split(mod_ctx[i], 6, axis=-1)
        a_lat = rms_norm(h_lat, norm_mix[i]) * (1.0 + sc_m) + sh_m
        a_ctx = rms_norm(h_ctx, norm_mix[i]) * (1.0 + csc_m) + csh_m
        if kind == 0:
            o_lat, o_ctx = rwkv7_mixer(a_lat, a_ctx, rw_mu[j], rw_wr[j], rw_wk[j], rw_wv[j], rw_wo[j],
                                       rw_w0[j], rw_w1[j], rw_w2[j], rw_a0[j], rw_a1[j], rw_a2[j],
                                       rw_g1[j], rw_g2[j], rw_kk[j], rw_ka[j], rw_rk[j],
                                       rw_lnx_g[j], rw_lnx_b[j], need_ctx)
        elif kind == 1:
            o_lat, o_ctx = gqa_mixer(a_lat, a_ctx, at_wq[j], at_wk[j], at_wv[j], at_wo[j],
                                     at_gq[j], at_gk[j], need_ctx)
        else:
            o_lat, o_ctx = na_mixer(a_lat, a_ctx, na_wqkv[j], na_wo[j], na_gq[j], na_gk[j],
                                    na_rpb[j], need_ctx)
        h_lat = h_lat + gt_m * o_lat
        f_lat = rms_norm(h_lat, norm_ffn[i]) * (1.0 + sc_f) + sh_f
        h_lat = h_lat + gt_f * swiglu(f_lat, ff_w1[i], ff_w3[i], ff_w2[i])
        if need_ctx:
            h_ctx = h_ctx + cgt_m * o_ctx
            f_ctx = rms_norm(h_ctx, norm_ffn[i]) * (1.0 + csc_f) + csh_f
            h_ctx = h_ctx + cgt_f * swiglu(f_ctx, ff_w1[i], ff_w3[i], ff_w2[i])
    return h_lat
```

```python
import functools
import math

import numpy as np
import jax
import jax.numpy as jnp
from jax import lax
from jax.experimental import pallas as pl
from jax.experimental.pallas import tpu as pltpu

F32 = jnp.float32
BF16 = jnp.bfloat16

GRID_W = 64
RW_HEAD = 64
AT_HEAD = 128
AT_KV_HEADS = 2
NA_HEAD = 64
WIN_H = 8
WIN_W = 16
ROPE_BASE = 10000.0
RMS_EPS = 1e-6
LNX_EPS = 64e-5
NORM_EPS = 1e-12
NEG_BIG = -1e30
EXP_M05 = math.exp(-0.5)

SCAN_CHUNK = 64
VMEM_LIMIT_BYTES_V7X = 48 * 1024 * 1024


def _params(*sem):
    return pltpu.CompilerParams(dimension_semantics=sem, vmem_limit_bytes=VMEM_LIMIT_BYTES_V7X)


def _sigmoid(x):
    return 1.0 / (1.0 + jnp.exp(-x))


def _rms(x, eps):
    return x * lax.rsqrt(jnp.mean(x * x, axis=-1, keepdims=True) + eps)


def _norm_mod(x, g, mod_ref, k):
    return (_rms(x, RMS_EPS) * g) * (1.0 + mod_ref[0, k + 1:k + 2, :]) + mod_ref[0, k:k + 1, :]


def _dot(a, b):
    return jnp.dot(a, b, preferred_element_type=F32)


def _dot_nt(a, b):
    return lax.dot_general(a, b, (((1,), (1,)), ((), ())), preferred_element_type=F32)


def _dot_tn(a, b):
    return lax.dot_general(a, b, (((0,), (0,)), ((), ())), preferred_element_type=F32)


def _split3(x):
    h1 = x.astype(BF16)
    r1 = x - h1.astype(F32)
    h2 = r1.astype(BF16)
    h3 = (r1 - h2.astype(F32)).astype(BF16)
    return h1, h2, h3


def _mod_index(bm):
    return (lambda b, *_: (b, 0, 0)) if bm > 1 else (lambda b, *_: (0, 0, 0))


def _mod_kernel(x_ref, w_ref, b_ref, o_ref):
    x = x_ref[...]
    s = x * _sigmoid(x)
    w = w_ref[0]
    sh, sm, _ = _split3(s)
    wh, wm, _ = _split3(w)
    o_ref[0] = _dot(sh, wh) + _dot(sh, wm) + _dot(sm, wh) + b_ref[0]


def _modulation(cc, mod_w, mod_b):
    depth, d, e = mod_w.shape
    tn = 1536
    return pl.pallas_call(
        _mod_kernel,
        grid=(depth, e // tn),
        in_specs=[pl.BlockSpec((8, d), lambda l, j: (0, 0)),
                  pl.BlockSpec((1, d, tn), lambda l, j: (l, 0, j)),
                  pl.BlockSpec((1, 1, tn), lambda l, j: (l, 0, j))],
        out_specs=pl.BlockSpec((1, 8, tn), lambda l, j: (l, 0, j)),
        out_shape=jax.ShapeDtypeStruct((depth, 8, e), F32),
        compiler_params=_params("parallel", "parallel"),
        name="modulation",
    )(cc, mod_w, mod_b.reshape(depth, 1, e))


def _ffn_kernel(x_ref, mod_ref, g_ref, w1_ref, w3_ref, w2_ref, o_ref, f_scr, acc_scr):
    j = pl.program_id(2)

    @pl.when(j == 0)
    def _():
        f_scr[...] = _norm_mod(x_ref[0], g_ref[...], mod_ref, 3).astype(BF16)
        acc_scr[...] = jnp.zeros_like(acc_scr)

    f = f_scr[...]
    h1 = _dot(f, w1_ref[...])
    h3 = _dot(f, w3_ref[...])
    hm = (h1 * _sigmoid(h1)) * h3
    acc_scr[...] += _dot(hm.astype(BF16), w2_ref[...])

    @pl.when(j == pl.num_programs(2) - 1)
    def _():
        o_ref[0] = x_ref[0] + mod_ref[0, 5:6, :] * acc_scr[...]


def _ffn(x, mod, g, w1, w3, w2, tm):
    b, n, d = x.shape
    f = w1.shape[1]
    tf = 256
    return pl.pallas_call(
        _ffn_kernel,
        grid=(b, n // tm, f // tf),
        in_specs=[pl.BlockSpec((1, tm, d), lambda b_, i, j: (b_, i, 0)),
                  pl.BlockSpec((1, 6, d), _mod_index(mod.shape[0])),
                  pl.BlockSpec((1, d), lambda b_, i, j: (0, 0)),
                  pl.BlockSpec((d, tf), lambda b_, i, j: (0, j)),
                  pl.BlockSpec((d, tf), lambda b_, i, j: (0, j)),
                  pl.BlockSpec((tf, d), lambda b_, i, j: (j, 0))],
        out_specs=pl.BlockSpec((1, tm, d), lambda b_, i, j: (b_, i, 0)),
        out_shape=jax.ShapeDtypeStruct((b, n, d), F32),
        scratch_shapes=[pltpu.VMEM((tm, d), BF16), pltpu.VMEM((tm, d), F32)],
        compiler_params=_params("parallel", "parallel", "arbitrary"),
        name="ffn",
    )(x, mod, g, w1, w3, w2)


def _oproj_kernel(x_ref, res_ref, mod_ref, w_ref, o_ref):
    o_ref[0] = res_ref[0] + mod_ref[0, 2:3, :] * _dot(x_ref[0], w_ref[...])


def _oproj(x, res, mod, w, tm):
    b, n, k = x.shape
    d = w.shape[1]
    return pl.pallas_call(
        _oproj_kernel,
        grid=(b, n // tm),
        in_specs=[pl.BlockSpec((1, tm, k), lambda b_, i: (b_, i, 0)),
                  pl.BlockSpec((1, tm, d), lambda b_, i: (b_, i, 0)),
                  pl.BlockSpec((1, 6, d), _mod_index(mod.shape[0])),
                  pl.BlockSpec((k, d), lambda b_, i: (0, 0))],
        out_specs=pl.BlockSpec((1, tm, d), lambda b_, i: (b_, i, 0)),
        out_shape=jax.ShapeDtypeStruct((b, n, d), F32),
        compiler_params=_params("parallel", "parallel"),
        name="oproj",
    )(x, res, mod, w)


def _gqa_qkv_kernel(x_ref, mod_ref, g_ref, w_ref, gq_ref, gk_ref, cos_ref, sin_ref,
                    q_ref, k_ref, v_ref, *, rope, n_q, n_kv, hd, scale):
    a = _norm_mod(x_ref[0], g_ref[...], mod_ref, 0).astype(BF16)
    qkv = _dot(a, w_ref[...])
    if rope:
        cos = cos_ref[...]
        sin = sin_ref[...]
        even = (lax.broadcasted_iota(jnp.int32, cos.shape, 1) % 2) == 0

    def head(xh, g):
        y = _rms(xh, RMS_EPS) * g
        if rope:
            partner = jnp.where(even, pltpu.roll(y, hd - 1, 1), pltpu.roll(y, 1, 1))
            y = y * cos + partner * sin
        return y

    gq = gq_ref[...]
    gk = gk_ref[...]
    for h in range(n_q):
        sl = slice(h * hd, (h + 1) * hd)
        q_ref[0, :, sl] = (head(qkv[:, sl], gq) * scale).astype(BF16)
    for h in range(n_kv):
        sl = slice(h * hd, (h + 1) * hd)
        k_ref[0, :, sl] = head(qkv[:, n_q * hd + h * hd:n_q * hd + (h + 1) * hd], gk).astype(BF16)
    v_ref[0] = qkv[:, (n_q + n_kv) * hd:].astype(BF16)


def _gqa_qkv(x, mod, g, w, gq, gk, cos, sin, tm, rope):
    b, n, d = x.shape
    hd = AT_HEAD
    n_kv = AT_KV_HEADS
    n_q = w.shape[1] // hd - 2 * n_kv
    kern = functools.partial(_gqa_qkv_kernel, rope=rope, n_q=n_q, n_kv=n_kv, hd=hd, scale=hd ** -0.5)
    tab = (lambda b_, i: (i, 0)) if rope else (lambda b_, i: (0, 0))
    return pl.pallas_call(
        kern,
        grid=(b, n // tm),
        in_specs=[pl.BlockSpec((1, tm, d), lambda b_, i: (b_, i, 0)),
                  pl.BlockSpec((1, 6, d), _mod_index(mod.shape[0])),
                  pl.BlockSpec((1, d), lambda b_, i: (0, 0)),
                  pl.BlockSpec(w.shape, lambda b_, i: (0, 0)),
                  pl.BlockSpec((1, hd), lambda b_, i: (0, 0)),
                  pl.BlockSpec((1, hd), lambda b_, i: (0, 0)),
                  pl.BlockSpec((tm, hd), tab),
                  pl.BlockSpec((tm, hd), tab)],
        out_specs=[pl.BlockSpec((1, tm, n_q * hd), lambda b_, i: (b_, i, 0)),
                   pl.BlockSpec((1, tm, n_kv * hd), lambda b_, i: (b_, i, 0)),
                   pl.BlockSpec((1, tm, n_kv * hd), lambda b_, i: (b_, i, 0))],
        out_shape=[jax.ShapeDtypeStruct((b, n, n_q * hd), BF16),
                   jax.ShapeDtypeStruct((b, n, n_kv * hd), BF16),
                   jax.ShapeDtypeStruct((b, n, n_kv * hd), BF16)],
        compiler_params=_params("parallel", "parallel"),
        name="gqa_qkv",
    )(x, mod, g, w, gq, gk, cos, sin)


def _flash_kernel(q_ref, k_ref, v_ref, o_ref, m_scr, l_scr, acc_scr, *, kvb, grp, hd):
    j = pl.program_id(3)

    @pl.when(j == 0)
    def _():
        m_scr[...] = jnp.full_like(m_scr, NEG_BIG)
        l_scr[...] = jnp.zeros_like(l_scr)
        acc_scr[...] = jnp.zeros_like(acc_scr)

    for kh in range(kvb):
        k = k_ref[0, :, kh * hd:(kh + 1) * hd]
        v = v_ref[0, :, kh * hd:(kh + 1) * hd]
        for g in range(grp):
            hq = kh * grp + g
            q = q_ref[0, :, hq * hd:(hq + 1) * hd]
            s = _dot_nt(q, k)
            m_prev = m_scr[hq]
            m_new = jnp.maximum(m_prev, jnp.max(s, axis=-1, keepdims=True))
            alpha = jnp.exp(m_prev - m_new)
            p = jnp.exp(s - m_new)
            l_scr[hq] = alpha * l_scr[hq] + jnp.sum(p, axis=-1, keepdims=True)
            acc_scr[hq] = alpha * acc_scr[hq] + _dot(p.astype(BF16), v)
            m_scr[hq] = m_new

    @pl.when(j == pl.num_programs(3) - 1)
    def _():
        for hq in range(kvb * grp):
            o_ref[0, :, hq * hd:(hq + 1) * hd] = (acc_scr[hq] / l_scr[hq]).astype(BF16)


def _flash(q, k, v, hd, grp, kvb, tq, tk):
    b, nq, dq = q.shape
    nk, dk = k.shape[1], k.shape[2]
    nblk = dk // (kvb * hd)
    kern = functools.partial(_flash_kernel, kvb=kvb, grp=grp, hd=hd)
    nh = kvb * grp
    return pl.pallas_call(
        kern,
        grid=(b, nblk, nq // tq, nk // tk),
        in_specs=[pl.BlockSpec((1, tq, nh * hd), lambda b_, g, i, j: (b_, i, g)),
                  pl.BlockSpec((1, tk, kvb * hd), lambda b_, g, i, j: (b_, j, g)),
                  pl.BlockSpec((1, tk, kvb * hd), lambda b_, g, i, j: (b_, j, g))],
        out_specs=pl.BlockSpec((1, tq, nh * hd), lambda b_, g, i, j: (b_, i, g)),
        out_shape=jax.ShapeDtypeStruct((b, nq, dq), BF16),
        scratch_shapes=[pltpu.VMEM((nh, tq, 1), F32), pltpu.VMEM((nh, tq, 1), F32),
                        pltpu.VMEM((nh, tq, hd), F32)],
        compiler_params=_params("parallel", "parallel", "parallel", "arbitrary"),
        name="flash_attention",
    )(q, k, v)


def _na_qkv_kernel(x_ref, mod_ref, g_ref, w_ref, gq_ref, gk_ref, q_ref, k_ref, v_ref, *, nh, hd, scale):
    a = _norm_mod(x_ref[0], g_ref[...], mod_ref, 0).astype(BF16)
    qkv = _dot(a, w_ref[...])
    gq = gq_ref[...]
    gk = gk_ref[...]
    for h in range(nh):
        sl = slice(h * hd, (h + 1) * hd)
        q_ref[0, :, sl] = (_rms(qkv[:, sl], RMS_EPS) * gq * scale).astype(BF16)
        k_ref[0, :, sl] = (_rms(qkv[:, nh * hd + h * hd:nh * hd + (h + 1) * hd], RMS_EPS) * gk).astype(BF16)
    v_ref[0] = qkv[:, 2 * nh * hd:].astype(BF16)


def _na_qkv(x, mod, g, w, gq, gk, tm):
    b, n, d = x.shape
    hd = NA_HEAD
    nh = w.shape[1] // (3 * hd)
    kern = functools.partial(_na_qkv_kernel, nh=nh, hd=hd, scale=hd ** -0.5)
    spec_o = pl.BlockSpec((1, tm, nh * hd), lambda b_, i: (b_, i, 0))
    return pl.pallas_call(
        kern,
        grid=(b, n // tm),
        in_specs=[pl.BlockSpec((1, tm, d), lambda b_, i: (b_, i, 0)),
                  pl.BlockSpec((1, 6, d), _mod_index(mod.shape[0])),
                  pl.BlockSpec((1, d), lambda b_, i: (0, 0)),
                  pl.BlockSpec(w.shape, lambda b_, i: (0, 0)),
                  pl.BlockSpec((1, hd), lambda b_, i: (0, 0)),
                  pl.BlockSpec((1, hd), lambda b_, i: (0, 0))],
        out_specs=[spec_o, spec_o, spec_o],
        out_shape=[jax.ShapeDtypeStruct((b, n, nh * hd), BF16)] * 3,
        compiler_params=_params("parallel", "parallel"),
        name="na_qkv",
    )(x, mod, g, w, gq, gk)


def _na_kernel(cls_ref, q_ref, k_ref, v_ref, kc_ref, vc_ref, bias_ref, o_ref, *, hb, hd, width, kh, rows):
    del cls_ref
    r = pl.program_id(2)
    rs = jnp.clip(r - kh // 2, 0, rows - kh)
    start = pl.multiple_of(rs * width, width)
    for h in range(hb):
        sl = slice(h * hd, (h + 1) * hd)
        q = q_ref[0, :, sl]
        kw = k_ref[0, pl.ds(start, kh * width), sl]
        vw = v_ref[0, pl.ds(start, kh * width), sl]
        s = _dot_nt(q, kw) + bias_ref[0, h]
        sc = _dot_nt(q, kc_ref[0, :, sl])
        m = jnp.maximum(jnp.max(s, axis=-1, keepdims=True), jnp.max(sc, axis=-1, keepdims=True))
        p = jnp.exp(s - m)
        pc = jnp.exp(sc - m)
        l = jnp.sum(p, axis=-1, keepdims=True) + jnp.sum(pc, axis=-1, keepdims=True)
        o = _dot(p.astype(BF16), vw) + _dot(pc.astype(BF16), vc_ref[0, :, sl])
        o_ref[0, :, sl] = (o / l).astype(BF16)


def _na_tables(rpb, rows):
    kh, kw = min(WIN_H, rows), WIN_W
    r_idx = np.arange(rows)
    delta = np.clip(r_idx - kh // 2, 0, rows - kh) - r_idx
    classes, cls_of_row = np.unique(delta, return_inverse=True)
    c_idx = np.arange(GRID_W)
    cs = np.clip(c_idx - kw // 2, 0, GRID_W - kw)
    kcol = np.arange(GRID_W)
    valid = (kcol[None, :] >= cs[:, None]) & (kcol[None, :] < cs[:, None] + kw)
    ri = classes[:, None] + np.arange(kh)[None, :] + (WIN_H - 1)
    ci = np.clip(kcol[None, :] - c_idx[:, None] + (WIN_W - 1), 0, 2 * WIN_W - 2)
    tab = rpb[:, ri[:, None, :, None], ci[None, :, None, :]]
    tab = jnp.where(valid[None, None, :, None, :], tab, NEG_BIG)
    tab = tab.transpose(1, 0, 2, 3, 4).reshape(len(classes), rpb.shape[0], GRID_W, kh * GRID_W)
    return tab.astype(F32), jnp.asarray(cls_of_row, jnp.int32), kh


def _na_attention(q, k, v, kc, vc, rpb):
    b, t, d = q.shape
    c = kc.shape[1]
    hd = NA_HEAD
    hb = 4
    rows = t // GRID_W
    tab, cls_of_row, kh = _na_tables(rpb, rows)
    kern = functools.partial(_na_kernel, hb=hb, hd=hd, width=GRID_W, kh=kh, rows=rows)
    grid_spec = pltpu.PrefetchScalarGridSpec(
        num_scalar_prefetch=1,
        grid=(b, d // (hb * hd), rows),
        in_specs=[pl.BlockSpec((1, GRID_W, hb * hd), lambda b_, g, r, cls: (b_, r, g)),
                  pl.BlockSpec((1, t, hb * hd), lambda b_, g, r, cls: (b_, 0, g)),
                  pl.BlockSpec((1, t, hb * hd), lambda b_, g, r, cls: (b_, 0, g)),
                  pl.BlockSpec((1, c, hb * hd), lambda b_, g, r, cls: (b_, 0, g)),
                  pl.BlockSpec((1, c, hb * hd), lambda b_, g, r, cls: (b_, 0, g)),
                  pl.BlockSpec((1, hb, GRID_W, kh * GRID_W), lambda b_, g, r, cls: (cls[r], g, 0, 0))],
        out_specs=pl.BlockSpec((1, GRID_W, hb * hd), lambda b_, g, r, cls: (b_, r, g)),
    )
    return pl.pallas_call(
        kern,
        grid_spec=grid_spec,
        out_shape=jax.ShapeDtypeStruct((b, t, d), BF16),
        compiler_params=_params("parallel", "parallel", "arbitrary"),
        name="na_attention",
    )(cls_of_row, q, k, v, kc, vc, tab)


def _rw_feat_kernel(x_ref, xp_ref, xn_ref, mod_ref, g_ref, mu_ref, wr_ref, wk_ref, wv_ref,
                    w1_ref, w2_ref, a1_ref, a2_ref, g1_ref, g2_ref, w0_ref, a0_ref,
                    r_ref, k_ref, v_ref, lw_ref, a_ref, gate_ref, *, tm, lora_w, lora_a):
    i = pl.program_id(1)
    g = g_ref[...]
    a = _norm_mod(x_ref[0], g, mod_ref, 0)
    a_prev = _norm_mod(xp_ref[0], g, mod_ref, 0)[7:8, :]
    a_next = _norm_mod(xn_ref[0], g, mod_ref, 0)[0:1, :]
    a_prev = jnp.where(i == 0, 0.0, a_prev)
    a_next = jnp.where(i == pl.num_programs(1) - 1, 0.0, a_next)
    row = lax.broadcasted_iota(jnp.int32, (tm, 1), 0)
    prev = jnp.where(row == 0, a_prev, pltpu.roll(a, 1, 0))
    nxt = jnp.where(row == tm - 1, a_next, pltpu.roll(a, tm - 1, 0))
    xx = 0.5 * (prev + nxt) - a

    def mix(j):
        return (a + xx * mu_ref[j:j + 1, :]).astype(BF16)

    r_ref[0] = _dot(mix(0), wr_ref[...])
    k_ref[0] = _dot(mix(2), wk_ref[...])
    v_ref[0] = _dot(mix(3), wv_ref[...])
    tw = jnp.tanh(_dot(mix(1), w1_ref[...]))
    ta = _dot(mix(4), a1_ref[...])
    for e in range(2):
        lora = _dot(tw[:, e * lora_w:(e + 1) * lora_w].astype(BF16), w2_ref[e])
        lw_ref[e, 0] = -EXP_M05 * _sigmoid(w0_ref[e:e + 1, :] + lora)
        la = _dot(ta[:, e * lora_a:(e + 1) * lora_a].astype(BF16), a2_ref[e])
        a_ref[e, 0] = _sigmoid(a0_ref[e:e + 1, :] + la)
    gate_ref[0] = _dot(_sigmoid(_dot(mix(5), g1_ref[...])).astype(BF16), g2_ref[...])


def _rw_features(x, mod, g, p, tm):
    b, n, d = x.shape
    nb8 = n // 8
    tb = tm // 8
    lora_w = p["w2"].shape[1]
    lora_a = p["a2"].shape[1]
    kern = functools.partial(_rw_feat_kernel, tm=tm, lora_w=lora_w, lora_a=lora_a)
    full = lambda arr: pl.BlockSpec(arr.shape, lambda b_, i: (0,) * arr.ndim)
    tile = pl.BlockSpec((1, tm, d), lambda b_, i: (b_, i, 0))
    tile2 = pl.BlockSpec((2, 1, tm, d), lambda b_, i: (0, b_, i, 0))
    names = ("mu", "wr", "wk", "wv", "w1", "w2", "a1", "a2", "g1", "g2", "w0", "a0")
    return pl.pallas_call(
        kern,
        grid=(b, n // tm),
        in_specs=[tile,
                  pl.BlockSpec((1, 8, d), lambda b_, i: (b_, jnp.maximum(i * tb - 1, 0), 0)),
                  pl.BlockSpec((1, 8, d), lambda b_, i: (b_, jnp.minimum((i + 1) * tb, nb8 - 1), 0)),
                  pl.BlockSpec((1, 6, d), _mod_index(mod.shape[0])),
                  pl.BlockSpec((1, d), lambda b_, i: (0, 0))] + [full(p[nm]) for nm in names],
        out_specs=[tile, tile, tile, tile2, tile2, tile],
        out_shape=[jax.ShapeDtypeStruct((b, n, d), F32)] * 3
        + [jax.ShapeDtypeStruct((2, b, n, d), F32)] * 2 + [jax.ShapeDtypeStruct((b, n, d), F32)],
        compiler_params=_params("parallel", "parallel"),
        name="rwkv_features",
    )(x, x, x, mod, g, *[p[nm] for nm in names])


def _rw_scan_kernel(s0_ref, r_ref, k_ref, v_ref, lw_ref, a_ref, kkp_ref, kap_ref, rkp_ref,
                    y_ref, bon_ref, sf_ref, s_scr, *, L, H, N):
    e = pl.program_id(0)
    c = pl.program_id(2)

    @pl.when(c == 0)
    def _():
        s_scr[...] = s0_ref[0, 0]

    rev = e == 1
    ri = lax.broadcasted_iota(jnp.int32, (L, L), 0)
    ci = lax.broadcasted_iota(jnp.int32, (L, L), 1)
    dist = jnp.where(rev, ci - ri, ri - ci)
    incl_b = jnp.where(dist >= 0, 1.0, 0.0).astype(BF16)
    eye = jnp.where(ri == ci, 1.0, 0.0).astype(F32)
    ri2 = lax.broadcasted_iota(jnp.int32, (2 * L, 2 * L), 0)
    ci2 = lax.broadcasted_iota(jnp.int32, (2 * L, 2 * L), 1)
    rr = ri2 & (L - 1)
    cc = ci2 & (L - 1)
    mask2 = jnp.where(rev, cc - rr, rr - cc) >= jnp.where(ri2 < L, 1, 0)

    lw = lw_ref[0, 0]
    l1, l2, l3 = _split3(lw)
    cw = _dot(incl_b, l1) + _dot(incl_b, l2) + _dot(incl_b, l3)
    tot = jnp.where(rev, cw[0:1, :], cw[L - 1:L, :])
    e_prev = jnp.exp(cw - lw)
    e_incl = jnp.exp(cw)
    e_neg = jnp.exp(-cw)
    e_rem = jnp.exp(tot - cw)
    e_tot = jnp.exp(tot)

    a = a_ref[0, 0]
    k = k_ref[0]
    r = r_ref[0]
    v = v_ref[0]
    kk_raw = k * kkp_ref[...]
    kd = k * (1.0 + (a - 1.0) * kap_ref[...])
    rk = r * kd * rkp_ref[...]
    r_t = r * e_incl
    k_t = kd * e_neg
    k_h = kd * e_rem
    n_fact = int(math.log2(L))

    for h in range(H):
        sl = slice(h * N, (h + 1) * N)
        kk = kk_raw[:, sl]
        kk = kk / jnp.maximum(jnp.sqrt(jnp.sum(kk * kk, axis=-1, keepdims=True)), NORM_EPS)
        b = kk * a[:, sl]
        x_ar = jnp.concatenate([-kk * e_prev[:, sl], r_t[:, sl]], axis=0).astype(BF16)
        z_bk = jnp.concatenate([b * e_neg[:, sl], k_t[:, sl]], axis=0).astype(BF16)
        gram = jnp.where(mask2, _dot_nt(x_ar, z_bk), 0.0)
        a_ab = gram[:L, :L]
        a_ak = gram[:L, L:]
        m_r = gram[L:, :]

        qb = a_ab.astype(BF16)
        q = _dot(qb, qb)
        t_inv = eye + a_ab
        for _ in range(n_fact - 2):
            qb = q.astype(BF16)
            st = _dot(jnp.concatenate([t_inv.astype(BF16), qb], axis=0), qb)
            t_inv = t_inv + st[:L]
            q = st[L:]
        t_inv = t_inv + _dot(t_inv.astype(BF16), q.astype(BF16))

        s_old = s_scr[h]
        xs = _dot_nt(x_ar, s_old.astype(BF16))
        vb = v[:, sl].astype(BF16)
        u = _dot(t_inv.astype(BF16), (xs[:L] + _dot(a_ak.astype(BF16), vb)).astype(BF16))
        uv = jnp.concatenate([u.astype(BF16), vb], axis=0)
        y_ref[0, 0, :, sl] = xs[L:] + _dot(m_r.astype(BF16), uv)
        bk = jnp.concatenate([b * e_rem[:, sl], k_h[:, sl]], axis=0).astype(BF16)
        s_scr[h] = s_old * e_tot[:, sl] + _dot_tn(uv, bk)
        bon_ref[0, 0, :, sl] = jnp.sum(rk[:, sl], axis=-1, keepdims=True) * v[:, sl]

    @pl.when(c == pl.num_programs(2) - 1)
    def _():
        sf_ref[0, 0] = s_scr[...]


def _rw_scan(s0, r, k, v, lw, a, kkp, kap, rkp):
    b, n, d = r.shape
    L = SCAN_CHUNK
    N = RW_HEAD
    H = d // N
    nc = n // L
    kern = functools.partial(_rw_scan_kernel, L=L, H=H, N=N)

    def chunk(e, c):
        return jnp.where(e == 1, nc - 1 - c, c)

    tok = pl.BlockSpec((1, L, d), lambda e, b_, c: (b_, chunk(e, c), 0))
    tok2 = pl.BlockSpec((1, 1, L, d), lambda e, b_, c: (e, b_, chunk(e, c), 0))
    vec = pl.BlockSpec((1, d), lambda e, b_, c: (0, 0))
    st = pl.BlockSpec((1, 1, H, N, N), lambda e, b_, c: (e, b_, 0, 0, 0))
    return pl.pallas_call(
        kern,
        grid=(2, b, nc),
        in_specs=[st, tok, tok, tok, tok2, tok2, vec, vec, vec],
        out_specs=[tok2, tok2, st],
        out_shape=[jax.ShapeDtypeStruct((2, b, n, d), F32), jax.ShapeDtypeStruct((2, b, n, d), F32),
                   jax.ShapeDtypeStruct((2, b, H, N, N), F32)],
        scratch_shapes=[pltpu.VMEM((H, N, N), F32)],
        compiler_params=_params("parallel", "parallel", "arbitrary"),
        name="rwkv_scan",
    )(s0, r, k, v, lw, a, kkp, kap, rkp)


def _rw_out_kernel(y_ref, bon_ref, gate_ref, res_ref, mod_ref, lg_ref, lb_ref, wo_ref, o_ref, yn_scr, *, H, N):
    y = y_ref[0, 0] + y_ref[1, 0]
    for h in range(H):
        sl = slice(h * N, (h + 1) * N)
        yh = y[:, sl]
        dlt = yh - jnp.mean(yh, axis=-1, keepdims=True)
        var = jnp.mean(dlt * dlt, axis=-1, keepdims=True)
        yn_scr[:, sl] = dlt * lax.rsqrt(var + LNX_EPS)
    z = (yn_scr[...] * lg_ref[...] + lb_ref[...] + (bon_ref[0, 0] + bon_ref[1, 0])) * gate_ref[0]
    o_ref[0] = res_ref[0] + mod_ref[0, 2:3, :] * _dot(z.astype(BF16), wo_ref[...])


def _rw_readout(y, bon, gate, res, mod, lg, lb, wo, tm):
    b, n, d = res.shape
    N = RW_HEAD
    kern = functools.partial(_rw_out_kernel, H=d // N, N=N)
    tile = pl.BlockSpec((1, tm, d), lambda b_, i: (b_, i, 0))
    tile2 = pl.BlockSpec((2, 1, tm, d), lambda b_, i: (0, b_, i, 0))
    vec = pl.BlockSpec((1, d), lambda b_, i: (0, 0))
    return pl.pallas_call(
        kern,
        grid=(b, n // tm),
        in_specs=[tile2, tile2, tile, tile, pl.BlockSpec((1, 6, d), _mod_index(mod.shape[0])),
                  vec, vec, pl.BlockSpec((d, d), lambda b_, i: (0, 0))],
        out_specs=tile,
        out_shape=jax.ShapeDtypeStruct((b, n, d), F32),
        scratch_shapes=[pltpu.VMEM((tm, d), F32)],
        compiler_params=_params("parallel", "parallel"),
        name="rwkv_readout",
    )(y, bon, gate, res, mod, lg, lb, wo)


def _rope_tables(n_tok, hd):
    t = jnp.arange(n_tok)
    rows = (t // GRID_W).astype(F32)
    cols = (t % GRID_W).astype(F32)
    d_axis = hd // 2
    inv = jnp.float32(ROPE_BASE) ** (-jnp.arange(0, d_axis, 2, dtype=F32) / d_axis)
    ang = jnp.concatenate([rows[:, None] * inv, cols[:, None] * inv], axis=-1)
    cos = jnp.repeat(jnp.cos(ang), 2, axis=-1)
    sign = jnp.tile(jnp.asarray([-1.0, 1.0], F32), hd // 2)
    sin = jnp.repeat(jnp.sin(ang), 2, axis=-1) * sign
    return cos, sin


def _tiles(n):
    return min(n, 512), min(n, 1024)


def kernel(x, c, ctx, c_ctx, mod_w, mod_b, norm_mix, norm_ffn, ff_w1, ff_w3, ff_w2, rw_mu, rw_wr, rw_wk, rw_wv, rw_wo, rw_w0, rw_w1, rw_w2, rw_a0, rw_a1, rw_a2, rw_g1, rw_g2, rw_kk, rw_ka, rw_rk, rw_lnx_g, rw_lnx_b, at_wq, at_wk, at_wv, at_wo, at_gq, at_gk, na_wqkv, na_wo, na_gq, na_gk, na_rpb):
    B, T, D = x.shape
    C = ctx.shape[1]
    depth = mod_w.shape[0]
    bf = lambda w: w.astype(BF16)

    cc = jnp.concatenate([c, c_ctx[None, :], jnp.zeros((8 - B - 1, D), F32)], axis=0)
    mods = _modulation(cc, mod_w, mod_b)
    mod_lat = mods[:, :B].reshape(depth, B, 6, D)
    mod_ctx = mods[:, B:B + 1].reshape(depth, 1, 6, D)

    tl, tl_ffn = _tiles(T)
    tc, tc_ffn = _tiles(C)
    h_lat, h_ctx = x, ctx
    for i in range(depth):
        need_ctx = i < depth - 1
        kind, j = i % 3, i // 3
        ml, mc = mod_lat[i], mod_ctx[i]
        g_mix = norm_mix[i][None, :]
        if kind == 0:
            cat = lambda w: jnp.concatenate([w[0], w[1]], axis=1)
            p = dict(mu=rw_mu[j], wr=bf(rw_wr[j]), wk=bf(rw_wk[j]), wv=bf(rw_wv[j]),
                     w1=bf(cat(rw_w1[j])), w2=bf(rw_w2[j]), a1=bf(cat(rw_a1[j])), a2=bf(rw_a2[j]),
                     g1=bf(rw_g1[j]), g2=bf(rw_g2[j]), w0=rw_w0[j], a0=rw_a0[j])
            kkp, kap, rkp = rw_kk[j][None, :], rw_ka[j][None, :], rw_rk[j].reshape(1, D)
            lg, lb, wo = rw_lnx_g[j][None, :], rw_lnx_b[j][None, :], bf(rw_wo[j])
            r_c, k_c, v_c, lw_c, a_c, gate_c = _rw_features(h_ctx, mc, g_mix, p, min(C, 256))
            r_l, k_l, v_l, lw_l, a_l, gate_l = _rw_features(h_lat, ml, g_mix, p, min(T, 256))
            s0 = jnp.zeros((2, B, D // RW_HEAD, RW_HEAD, RW_HEAD), F32)
            y_c, bon_c, s_c = _rw_scan(s0, r_c, k_c, v_c, lw_c, a_c, kkp, kap, rkp)
            y_l, bon_l, _ = _rw_scan(s_c, r_l, k_l, v_l, lw_l, a_l, kkp, kap, rkp)
            h_lat = _rw_readout(y_l, bon_l, gate_l, h_lat, ml, lg, lb, wo, min(T, 256))
            if need_ctx:
                h_ctx = _rw_readout(y_c, bon_c, gate_c, h_ctx, mc, lg, lb, wo, min(C, 256))
        elif kind == 1:
            w = bf(jnp.concatenate([at_wq[j], at_wk[j], at_wv[j]], axis=1))
            gq, gk, wo = at_gq[j][None, :], at_gk[j][None, :], bf(at_wo[j])
            cos, sin = _rope_tables(T, AT_HEAD)
            q_l, k_l, v_l = _gqa_qkv(h_lat, ml, g_mix, w, gq, gk, cos, sin, tl, True)
            q_c, k_c, v_c = _gqa_qkv(h_ctx, mc, g_mix, w, gq, gk, cos, sin, tc, False)
            k_all = jnp.concatenate([k_l, k_c], axis=1)
            v_all = jnp.concatenate([v_l, v_c], axis=1)
            grp = q_l.shape[2] // k_l.shape[2]
            o_l = _flash(q_l, k_all, v_all, AT_HEAD, grp, 1, tl, math.gcd(T + C, 256))
            h_lat = _oproj(o_l, h_lat, ml, wo, tl)
            if need_ctx:
                o_c = _flash(q_c, k_c, v_c, AT_HEAD, grp, 1, tc, tc)
                h_ctx = _oproj(o_c, h_ctx, mc, wo, tc)
        else:
            w, wo = bf(na_wqkv[j]), bf(na_wo[j])
            gq, gk = na_gq[j][None, :], na_gk[j][None, :]
            q_l, k_l, v_l = _na_qkv(h_lat, ml, g_mix, w, gq, gk, min(T, 256))
            q_c, k_c, v_c = _na_qkv(h_ctx, mc, g_mix, w, gq, gk, min(C, 256))
            o_l = _na_attention(q_l, k_l, v_l, k_c, v_c, na_rpb[j])
            h_lat = _oproj(o_l, h_lat, ml, wo, tl)
            if need_ctx:
                o_c = _flash(q_c, k_c, v_c, NA_HEAD, 1, 2, tc, tc)
                h_ctx = _oproj(o_c, h_ctx, mc, wo, tc)
        g_ffn = norm_ffn[i][None, :]
        w1, w3, w2 = bf(ff_w1[i]), bf(ff_w3[i]), bf(ff_w2[i])
        h_lat = _ffn(h_lat, ml, g_ffn, w1, w3, w2, tl_ffn)
        if need_ctx:
            h_ctx = _ffn(h_ctx, mc, g_ffn, w1, w3, w2, tc_ffn)
    return h_lat
```

```python
import functools
import math

import numpy as np
import jax
import jax.numpy as jnp
from jax import lax
from jax.experimental import pallas as pl
from jax.experimental.pallas import tpu as pltpu

F32 = jnp.float32
BF16 = jnp.bfloat16

GRID_W = 64
RW_HEAD = 64
AT_HEAD = 128
AT_KV_HEADS = 2
NA_HEAD = 64
WIN_H = 8
WIN_W = 16
ROPE_BASE = 10000.0
RMS_EPS = 1e-6
LNX_EPS = 64e-5
NORM_EPS = 1e-12
NEG_BIG = -1e30
EXP_M05 = math.exp(-0.5)

SCAN_CHUNK = 64
VMEM_LIMIT_BYTES_V7X = 48 * 1024 * 1024


def _params(*sem):
    return pltpu.CompilerParams(dimension_semantics=sem, vmem_limit_bytes=VMEM_LIMIT_BYTES_V7X)


def _sigmoid(x):
    return 1.0 / (1.0 + jnp.exp(-x))


def _rms(x, eps):
    return x * lax.rsqrt(jnp.mean(x * x, axis=-1, keepdims=True) + eps)


def _norm_mod(x, g, mod_ref, k):
    return (_rms(x, RMS_EPS) * g) * (1.0 + mod_ref[0, k + 1:k + 2, :]) + mod_ref[0, k:k + 1, :]


def _dot(a, b):
    return jnp.dot(a, b, preferred_element_type=F32)


def _dot_nt(a, b):
    return lax.dot_general(a, b, (((1,), (1,)), ((), ())), preferred_element_type=F32)


def _dot_tn(a, b):
    return lax.dot_general(a, b, (((0,), (0,)), ((), ())), preferred_element_type=F32)


def _split3(x):
    h1 = x.astype(BF16)
    r1 = x - h1.astype(F32)
    h2 = r1.astype(BF16)
    h3 = (r1 - h2.astype(F32)).astype(BF16)
    return h1, h2, h3


def _mod_index(bm):
    return (lambda b, *_: (b, 0, 0)) if bm > 1 else (lambda b, *_: (0, 0, 0))


def _mod_kernel(x_ref, w_ref, b_ref, o_ref):
    x = x_ref[...]
    s = x * _sigmoid(x)
    w = w_ref[0]
    sh, sm, _ = _split3(s)
    wh, wm, _ = _split3(w)
    o_ref[0] = _dot(sh, wh) + _dot(sh, wm) + _dot(sm, wh) + b_ref[0]


def _modulation(cc, mod_w, mod_b):
    depth, d, e = mod_w.shape
    tn = 1536
    return pl.pallas_call(
        _mod_kernel,
        grid=(depth, e // tn),
        in_specs=[pl.BlockSpec((8, d), lambda l, j: (0, 0)),
                  pl.BlockSpec((1, d, tn), lambda l, j: (l, 0, j)),
                  pl.BlockSpec((1, 1, tn), lambda l, j: (l, 0, j))],
        out_specs=pl.BlockSpec((1, 8, tn), lambda l, j: (l, 0, j)),
        out_shape=jax.ShapeDtypeStruct((depth, 8, e), F32),
        compiler_params=_params("parallel", "parallel"),
        name="modulation",
    )(cc, mod_w, mod_b.reshape(depth, 1, e))


def _ffn_kernel(x_ref, mod_ref, g_ref, w1_ref, w3_ref, w2_ref, o_ref, f_scr, acc_scr):
    j = pl.program_id(2)

    @pl.when(j == 0)
    def _():
        f_scr[...] = _norm_mod(x_ref[0], g_ref[...], mod_ref, 3).astype(BF16)
        acc_scr[...] = jnp.zeros_like(acc_scr)

    f = f_scr[...]
    h1 = _dot(f, w1_ref[...])
    h3 = _dot(f, w3_ref[...])
    hm = (h1 * _sigmoid(h1)) * h3
    acc_scr[...] += _dot(hm.astype(BF16), w2_ref[...])

    @pl.when(j == pl.num_programs(2) - 1)
    def _():
        o_ref[0] = x_ref[0] + mod_ref[0, 5:6, :] * acc_scr[...]


def _ffn(x, mod, g, w1, w3, w2, tm):
    b, n, d = x.shape
    f = w1.shape[1]
    tf = 256
    return pl.pallas_call(
        _ffn_kernel,
        grid=(b, n // tm, f // tf),
        in_specs=[pl.BlockSpec((1, tm, d), lambda b_, i, j: (b_, i, 0)),
                  pl.BlockSpec((1, 6, d), _mod_index(mod.shape[0])),
                  pl.BlockSpec((1, d), lambda b_, i, j: (0, 0)),
                  pl.BlockSpec((d, tf), lambda b_, i, j: (0, j)),
                  pl.BlockSpec((d, tf), lambda b_, i, j: (0, j)),
                  pl.BlockSpec((tf, d), lambda b_, i, j: (j, 0))],
        out_specs=pl.BlockSpec((1, tm, d), lambda b_, i, j: (b_, i, 0)),
        out_shape=jax.ShapeDtypeStruct((b, n, d), F32),
        scratch_shapes=[pltpu.VMEM((tm, d), BF16), pltpu.VMEM((tm, d), F32)],
        compiler_params=_params("parallel", "parallel", "arbitrary"),
        name="ffn",
    )(x, mod, g, w1, w3, w2)


def _oproj_kernel(x_ref, res_ref, mod_ref, w_ref, o_ref):
    o_ref[0] = res_ref[0] + mod_ref[0, 2:3, :] * _dot(x_ref[0], w_ref[...])


def _oproj(x, res, mod, w, tm):
    b, n, k = x.shape
    d = w.shape[1]
    return pl.pallas_call(
        _oproj_kernel,
        grid=(b, n // tm),
        in_specs=[pl.BlockSpec((1, tm, k), lambda b_, i: (b_, i, 0)),
                  pl.BlockSpec((1, tm, d), lambda b_, i: (b_, i, 0)),
                  pl.BlockSpec((1, 6, d), _mod_index(mod.shape[0])),
                  pl.BlockSpec((k, d), lambda b_, i: (0, 0))],
        out_specs=pl.BlockSpec((1, tm, d), lambda b_, i: (b_, i, 0)),
        out_shape=jax.ShapeDtypeStruct((b, n, d), F32),
        compiler_params=_params("parallel", "parallel"),
        name="oproj",
    )(x, res, mod, w)


def _gqa_qkv_kernel(x_ref, mod_ref, g_ref, w_ref, gq_ref, gk_ref, cos_ref, sin_ref,
                    q_ref, k_ref, v_ref, *, rope, n_q, n_kv, hd, scale):
    a = _norm_mod(x_ref[0], g_ref[...], mod_ref, 0).astype(BF16)
    qkv = _dot(a, w_ref[...])
    if rope:
        cos = cos_ref[...]
        sin = sin_ref[...]
        even = (lax.broadcasted_iota(jnp.int32, cos.shape, 1) % 2) == 0

    def head(xh, g):
        y = _rms(xh, RMS_EPS) * g
        if rope:
            partner = jnp.where(even, pltpu.roll(y, hd - 1, 1), pltpu.roll(y, 1, 1))
            y = y * cos + partner * sin
        return y

    gq = gq_ref[...]
    gk = gk_ref[...]
    for h in range(n_q):
        sl = slice(h * hd, (h + 1) * hd)
        q_ref[0, :, sl] = (head(qkv[:, sl], gq) * scale).astype(BF16)
    for h in range(n_kv):
        sl = slice(h * hd, (h + 1) * hd)
        k_ref[0, :, sl] = head(qkv[:, n_q * hd + h * hd:n_q * hd + (h + 1) * hd], gk).astype(BF16)
    v_ref[0] = qkv[:, (n_q + n_kv) * hd:].astype(BF16)


def _gqa_qkv(x, mod, g, w, gq, gk, cos, sin, tm, rope):
    b, n, d = x.shape
    hd = AT_HEAD
    n_kv = AT_KV_HEADS
    n_q = w.shape[1] // hd - 2 * n_kv
    kern = functools.partial(_gqa_qkv_kernel, rope=rope, n_q=n_q, n_kv=n_kv, hd=hd, scale=hd ** -0.5)
    tab = (lambda b_, i: (i, 0)) if rope else (lambda b_, i: (0, 0))
    return pl.pallas_call(
        kern,
        grid=(b, n // tm),
        in_specs=[pl.BlockSpec((1, tm, d), lambda b_, i: (b_, i, 0)),
                  pl.BlockSpec((1, 6, d), _mod_index(mod.shape[0])),
                  pl.BlockSpec((1, d), lambda b_, i: (0, 0)),
                  pl.BlockSpec(w.shape, lambda b_, i: (0, 0)),
                  pl.BlockSpec((1, hd), lambda b_, i: (0, 0)),
                  pl.BlockSpec((1, hd), lambda b_, i: (0, 0)),
                  pl.BlockSpec((tm, hd), tab),
                  pl.BlockSpec((tm, hd), tab)],
        out_specs=[pl.BlockSpec((1, tm, n_q * hd), lambda b_, i: (b_, i, 0)),
                   pl.BlockSpec((1, tm, n_kv * hd), lambda b_, i: (b_, i, 0)),
                   pl.BlockSpec((1, tm, n_kv * hd), lambda b_, i: (b_, i, 0))],
        out_shape=[jax.ShapeDtypeStruct((b, n, n_q * hd), BF16),
                   jax.ShapeDtypeStruct((b, n, n_kv * hd), BF16),
                   jax.ShapeDtypeStruct((b, n, n_kv * hd), BF16)],
        compiler_params=_params("parallel", "parallel"),
        name="gqa_qkv",
    )(x, mod, g, w, gq, gk, cos, sin)


def _flash_kernel(q_ref, k_ref, v_ref, o_ref, m_scr, l_scr, acc_scr, *, kvb, grp, hd):
    j = pl.program_id(3)

    @pl.when(j == 0)
    def _():
        m_scr[...] = jnp.full_like(m_scr, NEG_BIG)
        l_scr[...] = jnp.zeros_like(l_scr)
        acc_scr[...] = jnp.zeros_like(acc_scr)

    heads = [(kh, kh * grp + g) for kh in range(kvb) for g in range(grp)]
    s_l = [_dot_nt(k_ref[0, :, kh * hd:(kh + 1) * hd], q_ref[0, :, hq * hd:(hq + 1) * hd])
           for kh, hq in heads]
    p_l, alpha_l = [], []
    for (kh, hq), s in zip(heads, s_l):
        m_prev = m_scr[hq]
        m_new = jnp.maximum(m_prev, jnp.max(s, axis=0, keepdims=True))
        alpha = jnp.exp(m_prev - m_new)
        p = jnp.exp(s - m_new)
        l_scr[hq] = alpha * l_scr[hq] + jnp.sum(p, axis=0, keepdims=True)
        m_scr[hq] = m_new
        p_l.append(p.astype(BF16))
        alpha_l.append(alpha)
    for (kh, hq), p, alpha in zip(heads, p_l, alpha_l):
        pv = _dot_tn(v_ref[0, :, kh * hd:(kh + 1) * hd], p)
        acc_scr[hq] = alpha * acc_scr[hq] + pv

    @pl.when(j == pl.num_programs(3) - 1)
    def _():
        for hq in range(kvb * grp):
            o_ref[0, :, hq * hd:(hq + 1) * hd] = (acc_scr[hq] / l_scr[hq]).T.astype(BF16)


def _flash(q, k, v, hd, grp, kvb, tq, tk):
    b, nq, dq = q.shape
    nk, dk = k.shape[1], k.shape[2]
    nblk = dk // (kvb * hd)
    kern = functools.partial(_flash_kernel, kvb=kvb, grp=grp, hd=hd)
    nh = kvb * grp
    return pl.pallas_call(
        kern,
        grid=(b, nblk, nq // tq, nk // tk),
        in_specs=[pl.BlockSpec((1, tq, nh * hd), lambda b_, g, i, j: (b_, i, g)),
                  pl.BlockSpec((1, tk, kvb * hd), lambda b_, g, i, j: (b_, j, g)),
                  pl.BlockSpec((1, tk, kvb * hd), lambda b_, g, i, j: (b_, j, g))],
        out_specs=pl.BlockSpec((1, tq, nh * hd), lambda b_, g, i, j: (b_, i, g)),
        out_shape=jax.ShapeDtypeStruct((b, nq, dq), BF16),
        scratch_shapes=[pltpu.VMEM((nh, 1, tq), F32), pltpu.VMEM((nh, 1, tq), F32),
                        pltpu.VMEM((nh, hd, tq), F32)],
        compiler_params=_params("parallel", "parallel", "parallel", "arbitrary"),
        name="flash_attention",
    )(q, k, v)


def _na_qkv_kernel(x_ref, mod_ref, g_ref, w_ref, gq_ref, gk_ref, q_ref, k_ref, v_ref, *, nh, hd, scale):
    a = _norm_mod(x_ref[0], g_ref[...], mod_ref, 0).astype(BF16)
    qkv = _dot(a, w_ref[...])
    gq = gq_ref[...]
    gk = gk_ref[...]
    for h in range(nh):
        sl = slice(h * hd, (h + 1) * hd)
        q_ref[0, :, sl] = (_rms(qkv[:, sl], RMS_EPS) * gq * scale).astype(BF16)
        k_ref[0, :, sl] = (_rms(qkv[:, nh * hd + h * hd:nh * hd + (h + 1) * hd], RMS_EPS) * gk).astype(BF16)
    v_ref[0] = qkv[:, 2 * nh * hd:].astype(BF16)


def _na_qkv(x, mod, g, w, gq, gk, tm):
    b, n, d = x.shape
    hd = NA_HEAD
    nh = w.shape[1] // (3 * hd)
    kern = functools.partial(_na_qkv_kernel, nh=nh, hd=hd, scale=hd ** -0.5)
    spec_o = pl.BlockSpec((1, tm, nh * hd), lambda b_, i: (b_, i, 0))
    return pl.pallas_call(
        kern,
        grid=(b, n // tm),
        in_specs=[pl.BlockSpec((1, tm, d), lambda b_, i: (b_, i, 0)),
                  pl.BlockSpec((1, 6, d), _mod_index(mod.shape[0])),
                  pl.BlockSpec((1, d), lambda b_, i: (0, 0)),
                  pl.BlockSpec(w.shape, lambda b_, i: (0, 0)),
                  pl.BlockSpec((1, hd), lambda b_, i: (0, 0)),
                  pl.BlockSpec((1, hd), lambda b_, i: (0, 0))],
        out_specs=[spec_o, spec_o, spec_o],
        out_shape=[jax.ShapeDtypeStruct((b, n, nh * hd), BF16)] * 3,
        compiler_params=_params("parallel", "parallel"),
        name="na_qkv",
    )(x, mod, g, w, gq, gk)


def _na_kernel(cls_ref, q_ref, k_ref, v_ref, kc_ref, vc_ref, bias_ref, o_ref, *, hb, hd, width, kh, rows):
    del cls_ref
    r = pl.program_id(2)
    rs = jnp.clip(r - kh // 2, 0, rows - kh)
    start = pl.multiple_of(rs * width, width)
    win = pl.ds(start, kh * width)
    sls = [slice(h * hd, (h + 1) * hd) for h in range(hb)]
    s_l = [_dot_nt(q_ref[0, :, sl], k_ref[0, win, sl]) for sl in sls]
    sc_l = [_dot_nt(q_ref[0, :, sl], kc_ref[0, :, sl]) for sl in sls]
    p_l, pc_l, l_l = [], [], []
    for h in range(hb):
        s = s_l[h] + bias_ref[0, h]
        sc = sc_l[h]
        m = jnp.maximum(jnp.max(s, axis=-1, keepdims=True), jnp.max(sc, axis=-1, keepdims=True))
        p = jnp.exp(s - m)
        pc = jnp.exp(sc - m)
        l_l.append(jnp.sum(p, axis=-1, keepdims=True) + jnp.sum(pc, axis=-1, keepdims=True))
        p_l.append(p.astype(BF16))
        pc_l.append(pc.astype(BF16))
    o_l = [_dot(p_l[h], v_ref[0, win, sls[h]]) + _dot(pc_l[h], vc_ref[0, :, sls[h]]) for h in range(hb)]
    for h in range(hb):
        o_ref[0, :, sls[h]] = (o_l[h] / l_l[h]).astype(BF16)


def _na_tables(rpb, rows):
    kh, kw = min(WIN_H, rows), WIN_W
    r_idx = np.arange(rows)
    delta = np.clip(r_idx - kh // 2, 0, rows - kh) - r_idx
    classes, cls_of_row = np.unique(delta, return_inverse=True)
    c_idx = np.arange(GRID_W)
    cs = np.clip(c_idx - kw // 2, 0, GRID_W - kw)
    kcol = np.arange(GRID_W)
    valid = (kcol[None, :] >= cs[:, None]) & (kcol[None, :] < cs[:, None] + kw)
    pad = GRID_W
    rp = jnp.pad(rpb.astype(F32), ((0, 0), (0, 0), (pad, pad)))
    toep = jnp.stack([rp[:, :, pad + WIN_W - 1 - c:pad + WIN_W - 1 - c + GRID_W] for c in range(GRID_W)], axis=2)
    toep = jnp.where(valid[None, None, :, :], toep, NEG_BIG)
    tab = jnp.stack([toep[:, int(dl) + WIN_H - 1:int(dl) + WIN_H - 1 + kh] for dl in classes], axis=0)
    tab = tab.transpose(0, 1, 3, 2, 4).reshape(len(classes), rpb.shape[0], GRID_W, kh * GRID_W)
    return tab, jnp.asarray(cls_of_row, jnp.int32), kh


def _na_attention(q, k, v, kc, vc, rpb):
    b, t, d = q.shape
    c = kc.shape[1]
    hd = NA_HEAD
    hb = 4
    rows = t // GRID_W
    tab, cls_of_row, kh = _na_tables(rpb, rows)
    kern = functools.partial(_na_kernel, hb=hb, hd=hd, width=GRID_W, kh=kh, rows=rows)
    grid_spec = pltpu.PrefetchScalarGridSpec(
        num_scalar_prefetch=1,
        grid=(b, d // (hb * hd), rows),
        in_specs=[pl.BlockSpec((1, GRID_W, hb * hd), lambda b_, g, r, cls: (b_, r, g)),
                  pl.BlockSpec((1, t, hb * hd), lambda b_, g, r, cls: (b_, 0, g)),
                  pl.BlockSpec((1, t, hb * hd), lambda b_, g, r, cls: (b_, 0, g)),
                  pl.BlockSpec((1, c, hb * hd), lambda b_, g, r, cls: (b_, 0, g)),
                  pl.BlockSpec((1, c, hb * hd), lambda b_, g, r, cls: (b_, 0, g)),
                  pl.BlockSpec((1, hb, GRID_W, kh * GRID_W), lambda b_, g, r, cls: (cls[r], g, 0, 0))],
        out_specs=pl.BlockSpec((1, GRID_W, hb * hd), lambda b_, g, r, cls: (b_, r, g)),
    )
    return pl.pallas_call(
        kern,
        grid_spec=grid_spec,
        out_shape=jax.ShapeDtypeStruct((b, t, d), BF16),
        compiler_params=_params("parallel", "parallel", "arbitrary"),
        name="na_attention",
    )(cls_of_row, q, k, v, kc, vc, tab)


def _rw_feat_kernel(x_ref, xp_ref, xn_ref, mod_ref, g_ref, mu_ref, wr_ref, wk_ref, wv_ref,
                    w1_ref, w2_ref, a1_ref, a2_ref, g1_ref, g2_ref, w0_ref, a0_ref,
                    r_ref, k_ref, v_ref, lw_ref, a_ref, gate_ref, *, tm, lora_w, lora_a):
    i = pl.program_id(1)
    g = g_ref[...]
    a = _norm_mod(x_ref[0], g, mod_ref, 0)
    a_prev = _norm_mod(xp_ref[0], g, mod_ref, 0)[7:8, :]
    a_next = _norm_mod(xn_ref[0], g, mod_ref, 0)[0:1, :]
    a_prev = jnp.where(i == 0, 0.0, a_prev)
    a_next = jnp.where(i == pl.num_programs(1) - 1, 0.0, a_next)
    row = lax.broadcasted_iota(jnp.int32, (tm, 1), 0)
    prev = jnp.where(row == 0, a_prev, pltpu.roll(a, 1, 0))
    nxt = jnp.where(row == tm - 1, a_next, pltpu.roll(a, tm - 1, 0))
    xx = 0.5 * (prev + nxt) - a

    def mix(j):
        return (a + xx * mu_ref[j:j + 1, :]).astype(BF16)

    r_ref[0] = _dot(mix(0), wr_ref[...])
    k_ref[0] = _dot(mix(2), wk_ref[...])
    v_ref[0] = _dot(mix(3), wv_ref[...])
    tw = jnp.tanh(_dot(mix(1), w1_ref[...]))
    ta = _dot(mix(4), a1_ref[...])
    for e in range(2):
        lora = _dot(tw[:, e * lora_w:(e + 1) * lora_w].astype(BF16), w2_ref[e])
        lw_ref[e, 0] = -EXP_M05 * _sigmoid(w0_ref[e:e + 1, :] + lora)
        la = _dot(ta[:, e * lora_a:(e + 1) * lora_a].astype(BF16), a2_ref[e])
        a_ref[e, 0] = _sigmoid(a0_ref[e:e + 1, :] + la)
    gate_ref[0] = _dot(_sigmoid(_dot(mix(5), g1_ref[...])).astype(BF16), g2_ref[...])


def _rw_features(x, mod, g, p, tm):
    b, n, d = x.shape
    nb8 = n // 8
    tb = tm // 8
    lora_w = p["w2"].shape[1]
    lora_a = p["a2"].shape[1]
    kern = functools.partial(_rw_feat_kernel, tm=tm, lora_w=lora_w, lora_a=lora_a)
    full = lambda arr: pl.BlockSpec(arr.shape, lambda b_, i: (0,) * arr.ndim)
    tile = pl.BlockSpec((1, tm, d), lambda b_, i: (b_, i, 0))
    tile2 = pl.BlockSpec((2, 1, tm, d), lambda b_, i: (0, b_, i, 0))
    names = ("mu", "wr", "wk", "wv", "w1", "w2", "a1", "a2", "g1", "g2", "w0", "a0")
    return pl.pallas_call(
        kern,
        grid=(b, n // tm),
        in_specs=[tile,
                  pl.BlockSpec((1, 8, d), lambda b_, i: (b_, jnp.maximum(i * tb - 1, 0), 0)),
                  pl.BlockSpec((1, 8, d), lambda b_, i: (b_, jnp.minimum((i + 1) * tb, nb8 - 1), 0)),
                  pl.BlockSpec((1, 6, d), _mod_index(mod.shape[0])),
                  pl.BlockSpec((1, d), lambda b_, i: (0, 0))] + [full(p[nm]) for nm in names],
        out_specs=[tile, tile, tile, tile2, tile2, tile],
        out_shape=[jax.ShapeDtypeStruct((b, n, d), F32)] * 3
        + [jax.ShapeDtypeStruct((2, b, n, d), F32)] * 2 + [jax.ShapeDtypeStruct((b, n, d), F32)],
        compiler_params=_params("parallel", "parallel"),
        name="rwkv_features",
    )(x, x, x, mod, g, *[p[nm] for nm in names])


def _rw_scan_kernel(s0_ref, r_ref, k_ref, v_ref, lw_ref, a_ref, kkp_ref, kap_ref, rkp_ref,
                    y_ref, bon_ref, sf_ref, s_scr, *, L, H, N):
    e = pl.program_id(0)
    c = pl.program_id(2)

    @pl.when(c == 0)
    def _():
        s_scr[...] = s0_ref[0, 0]

    rev = e == 1
    ri = lax.broadcasted_iota(jnp.int32, (L, L), 0)
    ci = lax.broadcasted_iota(jnp.int32, (L, L), 1)
    dist = jnp.where(rev, ci - ri, ri - ci)
    incl_b = jnp.where(dist >= 0, 1.0, 0.0).astype(BF16)
    eye = jnp.where(ri == ci, 1.0, 0.0).astype(F32)
    ri2 = lax.broadcasted_iota(jnp.int32, (2 * L, 2 * L), 0)
    ci2 = lax.broadcasted_iota(jnp.int32, (2 * L, 2 * L), 1)
    rr = ri2 & (L - 1)
    cc = ci2 & (L - 1)
    mask2 = jnp.where(rev, cc - rr, rr - cc) >= jnp.where(ri2 < L, 1, 0)

    lw = lw_ref[0, 0]
    l1, l2, l3 = _split3(lw)
    cw = _dot(incl_b, l1) + _dot(incl_b, l2) + _dot(incl_b, l3)
    tot = jnp.where(rev, cw[0:1, :], cw[L - 1:L, :])
    e_prev = jnp.exp(cw - lw)
    e_incl = jnp.exp(cw)
    e_neg = jnp.exp(-cw)
    e_rem = jnp.exp(tot - cw)
    e_tot = jnp.exp(tot)

    a = a_ref[0, 0]
    k = k_ref[0]
    r = r_ref[0]
    v = v_ref[0]
    kk_raw = k * kkp_ref[...]
    kd = k * (1.0 + (a - 1.0) * kap_ref[...])
    rk = r * kd * rkp_ref[...]
    r_t = r * e_incl
    k_t = kd * e_neg
    k_h = kd * e_rem
    n_fact = int(math.log2(L))

    heads = range(H)
    sls = [slice(h * N, (h + 1) * N) for h in heads]
    x_ar, z_bk, b_l, vb = [], [], [], []
    for sl in sls:
        kk = kk_raw[:, sl]
        kk = kk / jnp.maximum(jnp.sqrt(jnp.sum(kk * kk, axis=-1, keepdims=True)), NORM_EPS)
        b = kk * a[:, sl]
        b_l.append(b)
        x_ar.append(jnp.concatenate([-kk * e_prev[:, sl], r_t[:, sl]], axis=0).astype(BF16))
        z_bk.append(jnp.concatenate([b * e_neg[:, sl], k_t[:, sl]], axis=0).astype(BF16))
        vb.append(v[:, sl].astype(BF16))
    gram = [jnp.where(mask2, _dot_nt(x_ar[h], z_bk[h]), 0.0) for h in heads]
    s_old = [s_scr[h] for h in heads]
    xs = [_dot_nt(x_ar[h], s_old[h].astype(BF16)) for h in heads]
    rhs = [xs[h][:L] + _dot(gram[h][:L, L:].astype(BF16), vb[h]) for h in heads]

    qb = [gram[h][:L, :L].astype(BF16) for h in heads]
    t_inv = [eye + gram[h][:L, :L] for h in heads]
    q = [_dot(qb[h], qb[h]) for h in heads]
    for _ in range(n_fact - 2):
        qb = [q[h].astype(BF16) for h in heads]
        st = [_dot(jnp.concatenate([t_inv[h].astype(BF16), qb[h]], axis=0), qb[h]) for h in heads]
        t_inv = [t_inv[h] + st[h][:L] for h in heads]
        q = [st[h][L:] for h in heads]
    st = [_dot(t_inv[h].astype(BF16), q[h].astype(BF16)) for h in heads]
    t_inv = [t_inv[h] + st[h] for h in heads]

    u = [_dot(t_inv[h].astype(BF16), rhs[h].astype(BF16)) for h in heads]
    uv = [jnp.concatenate([u[h].astype(BF16), vb[h]], axis=0) for h in heads]
    y = [xs[h][L:] + _dot(gram[h][L:, :].astype(BF16), uv[h]) for h in heads]
    bk = [jnp.concatenate([b_l[h] * e_rem[:, sls[h]], k_h[:, sls[h]]], axis=0).astype(BF16) for h in heads]
    s_new = [s_old[h] * e_tot[:, sls[h]] + _dot_tn(uv[h], bk[h]) for h in heads]
    for h in heads:
        sl = sls[h]
        y_ref[0, 0, :, sl] = y[h]
        s_scr[h] = s_new[h]
        bon_ref[0, 0, :, sl] = jnp.sum(rk[:, sl], axis=-1, keepdims=True) * v[:, sl]

    @pl.when(c == pl.num_programs(2) - 1)
    def _():
        sf_ref[0, 0] = s_scr[...]


def _rw_scan(s0, r, k, v, lw, a, kkp, kap, rkp):
    b, n, d = r.shape
    L = SCAN_CHUNK
    N = RW_HEAD
    H = d // N
    nc = n // L
    kern = functools.partial(_rw_scan_kernel, L=L, H=H, N=N)

    def chunk(e, c):
        return jnp.where(e == 1, nc - 1 - c, c)

    tok = pl.BlockSpec((1, L, d), lambda e, b_, c: (b_, chunk(e, c), 0))
    tok2 = pl.BlockSpec((1, 1, L, d), lambda e, b_, c: (e, b_, chunk(e, c), 0))
    vec = pl.BlockSpec((1, d), lambda e, b_, c: (0, 0))
    st = pl.BlockSpec((1, 1, H, N, N), lambda e, b_, c: (e, b_, 0, 0, 0))
    return pl.pallas_call(
        kern,
        grid=(2, b, nc),
        in_specs=[st, tok, tok, tok, tok2, tok2, vec, vec, vec],
        out_specs=[tok2, tok2, st],
        out_shape=[jax.ShapeDtypeStruct((2, b, n, d), F32), jax.ShapeDtypeStruct((2, b, n, d), F32),
                   jax.ShapeDtypeStruct((2, b, H, N, N), F32)],
        scratch_shapes=[pltpu.VMEM((H, N, N), F32)],
        compiler_params=_params("parallel", "parallel", "arbitrary"),
        name="rwkv_scan",
    )(s0, r, k, v, lw, a, kkp, kap, rkp)


def _rw_out_kernel(y_ref, bon_ref, gate_ref, res_ref, mod_ref, lg_ref, lb_ref, wo_ref, o_ref, yn_scr, *, H, N):
    y = y_ref[0, 0] + y_ref[1, 0]
    for h in range(H):
        sl = slice(h * N, (h + 1) * N)
        yh = y[:, sl]
        dlt = yh - jnp.mean(yh, axis=-1, keepdims=True)
        var = jnp.mean(dlt * dlt, axis=-1, keepdims=True)
        yn_scr[:, sl] = dlt * lax.rsqrt(var + LNX_EPS)
    z = (yn_scr[...] * lg_ref[...] + lb_ref[...] + (bon_ref[0, 0] + bon_ref[1, 0])) * gate_ref[0]
    o_ref[0] = res_ref[0] + mod_ref[0, 2:3, :] * _dot(z.astype(BF16), wo_ref[...])


def _rw_readout(y, bon, gate, res, mod, lg, lb, wo, tm):
    b, n, d = res.shape
    N = RW_HEAD
    kern = functools.partial(_rw_out_kernel, H=d // N, N=N)
    tile = pl.BlockSpec((1, tm, d), lambda b_, i: (b_, i, 0))
    tile2 = pl.BlockSpec((2, 1, tm, d), lambda b_, i: (0, b_, i, 0))
    vec = pl.BlockSpec((1, d), lambda b_, i: (0, 0))
    return pl.pallas_call(
        kern,
        grid=(b, n // tm),
        in_specs=[tile2, tile2, tile, tile, pl.BlockSpec((1, 6, d), _mod_index(mod.shape[0])),
                  vec, vec, pl.BlockSpec((d, d), lambda b_, i: (0, 0))],
        out_specs=tile,
        out_shape=jax.ShapeDtypeStruct((b, n, d), F32),
        scratch_shapes=[pltpu.VMEM((tm, d), F32)],
        compiler_params=_params("parallel", "parallel"),
        name="rwkv_readout",
    )(y, bon, gate, res, mod, lg, lb, wo)


def _rope_tables(n_tok, hd):
    t = jnp.arange(n_tok)
    rows = (t // GRID_W).astype(F32)
    cols = (t % GRID_W).astype(F32)
    d_axis = hd // 2
    inv = jnp.float32(ROPE_BASE) ** (-jnp.arange(0, d_axis, 2, dtype=F32) / d_axis)
    ang = jnp.concatenate([rows[:, None] * inv, cols[:, None] * inv], axis=-1)
    cos = jnp.repeat(jnp.cos(ang), 2, axis=-1)
    sign = jnp.tile(jnp.asarray([-1.0, 1.0], F32), hd // 2)
    sin = jnp.repeat(jnp.sin(ang), 2, axis=-1) * sign
    return cos, sin


def _tiles(n):
    return min(n, 512), min(n, 1024)


def kernel(x, c, ctx, c_ctx, mod_w, mod_b, norm_mix, norm_ffn, ff_w1, ff_w3, ff_w2, rw_mu, rw_wr, rw_wk, rw_wv, rw_wo, rw_w0, rw_w1, rw_w2, rw_a0, rw_a1, rw_a2, rw_g1, rw_g2, rw_kk, rw_ka, rw_rk, rw_lnx_g, rw_lnx_b, at_wq, at_wk, at_wv, at_wo, at_gq, at_gk, na_wqkv, na_wo, na_gq, na_gk, na_rpb):
    B, T, D = x.shape
    C = ctx.shape[1]
    depth = mod_w.shape[0]
    bf = lambda w: w.astype(BF16)

    cc = jnp.concatenate([c, c_ctx[None, :], jnp.zeros((8 - B - 1, D), F32)], axis=0)
    mods = _modulation(cc, mod_w, mod_b)
    mod_lat = mods[:, :B].reshape(depth, B, 6, D)
    mod_ctx = mods[:, B:B + 1].reshape(depth, 1, 6, D)

    tl, tl_ffn = _tiles(T)
    tc, tc_ffn = _tiles(C)
    h_lat, h_ctx = x, ctx
    for i in range(depth):
        need_ctx = i < depth - 1
        kind, j = i % 3, i // 3
        ml, mc = mod_lat[i], mod_ctx[i]
        g_mix = norm_mix[i][None, :]
        if kind == 0:
            cat = lambda w: jnp.concatenate([w[0], w[1]], axis=1)
            p = dict(mu=rw_mu[j], wr=bf(rw_wr[j]), wk=bf(rw_wk[j]), wv=bf(rw_wv[j]),
                     w1=bf(cat(rw_w1[j])), w2=bf(rw_w2[j]), a1=bf(cat(rw_a1[j])), a2=bf(rw_a2[j]),
                     g1=bf(rw_g1[j]), g2=bf(rw_g2[j]), w0=rw_w0[j], a0=rw_a0[j])
            kkp, kap, rkp = rw_kk[j][None, :], rw_ka[j][None, :], rw_rk[j].reshape(1, D)
            lg, lb, wo = rw_lnx_g[j][None, :], rw_lnx_b[j][None, :], bf(rw_wo[j])
            r_c, k_c, v_c, lw_c, a_c, gate_c = _rw_features(h_ctx, mc, g_mix, p, min(C, 256))
            r_l, k_l, v_l, lw_l, a_l, gate_l = _rw_features(h_lat, ml, g_mix, p, min(T, 256))
            s0 = jnp.zeros((2, B, D // RW_HEAD, RW_HEAD, RW_HEAD), F32)
            y_c, bon_c, s_c = _rw_scan(s0, r_c, k_c, v_c, lw_c, a_c, kkp, kap, rkp)
            y_l, bon_l, _ = _rw_scan(s_c, r_l, k_l, v_l, lw_l, a_l, kkp, kap, rkp)
            h_lat = _rw_readout(y_l, bon_l, gate_l, h_lat, ml, lg, lb, wo, min(T, 256))
            if need_ctx:
                h_ctx = _rw_readout(y_c, bon_c, gate_c, h_ctx, mc, lg, lb, wo, min(C, 256))
        elif kind == 1:
            w = bf(jnp.concatenate([at_wq[j], at_wk[j], at_wv[j]], axis=1))
            gq, gk, wo = at_gq[j][None, :], at_gk[j][None, :], bf(at_wo[j])
            cos, sin = _rope_tables(T, AT_HEAD)
            q_l, k_l, v_l = _gqa_qkv(h_lat, ml, g_mix, w, gq, gk, cos, sin, tl, True)
            q_c, k_c, v_c = _gqa_qkv(h_ctx, mc, g_mix, w, gq, gk, cos, sin, tc, False)
            k_all = jnp.concatenate([k_l, k_c], axis=1)
            v_all = jnp.concatenate([v_l, v_c], axis=1)
            grp = q_l.shape[2] // k_l.shape[2]
            o_l = _flash(q_l, k_all, v_all, AT_HEAD, grp, 1, tl, math.gcd(T + C, 256))
            h_lat = _oproj(o_l, h_lat, ml, wo, tl)
            if need_ctx:
                o_c = _flash(q_c, k_c, v_c, AT_HEAD, grp, 1, tc, tc)
                h_ctx = _oproj(o_c, h_ctx, mc, wo, tc)
        else:
            w, wo = bf(na_wqkv[j]), bf(na_wo[j])
            gq, gk = na_gq[j][None, :], na_gk[j][None, :]
            q_l, k_l, v_l = _na_qkv(h_lat, ml, g_mix, w, gq, gk, min(T, 256))
            q_c, k_c, v_c = _na_qkv(h_ctx, mc, g_mix, w, gq, gk, min(C, 256))
            o_l = _na_attention(q_l, k_l, v_l, k_c, v_c, na_rpb[j])
            h_lat = _oproj(o_l, h_lat, ml, wo, tl)
            if need_ctx:
                o_c = _flash(q_c, k_c, v_c, NA_HEAD, 1, 2, tc, tc)
                h_ctx = _oproj(o_c, h_ctx, mc, wo, tc)
        g_ffn = norm_ffn[i][None, :]
        w1, w3, w2 = bf(ff_w1[i]), bf(ff_w3[i]), bf(ff_w2[i])
        h_lat = _ffn(h_lat, ml, g_ffn, w1, w3, w2, tl_ffn)
        if need_ctx:
            h_ctx = _ffn(h_ctx, mc, g_ffn, w1, w3, w2, tc_ffn)
    return h_lat
```

```python
import functools
import math

import numpy as np
import jax
import jax.numpy as jnp
from jax import lax
from jax.experimental import pallas as pl
from jax.experimental.pallas import tpu as pltpu

F32 = jnp.float32
BF16 = jnp.bfloat16

GRID_W = 64
RW_HEAD = 64
AT_HEAD = 128
AT_KV_HEADS = 2
NA_HEAD = 64
WIN_H = 8
WIN_W = 16
ROPE_BASE = 10000.0
RMS_EPS = 1e-6
LNX_EPS = 64e-5
NORM_EPS = 1e-12
NEG_BIG = -1e30
EXP_M05 = math.exp(-0.5)

SCAN_CHUNK = 64
VMEM_LIMIT_BYTES_V7X = 48 * 1024 * 1024


def _params(*sem):
    return pltpu.CompilerParams(dimension_semantics=sem, vmem_limit_bytes=VMEM_LIMIT_BYTES_V7X)


def _sigmoid(x):
    return 1.0 / (1.0 + jnp.exp(-x))


def _rms(x, eps):
    return x * lax.rsqrt(jnp.mean(x * x, axis=-1, keepdims=True) + eps)


def _norm_mod(x, g, mod_ref, k):
    return (_rms(x, RMS_EPS) * g) * (1.0 + mod_ref[0, k + 1:k + 2, :]) + mod_ref[0, k:k + 1, :]


def _dot(a, b):
    return jnp.dot(a, b, preferred_element_type=F32)


def _dot_nt(a, b):
    return lax.dot_general(a, b, (((1,), (1,)), ((), ())), preferred_element_type=F32)


def _dot_tn(a, b):
    return lax.dot_general(a, b, (((0,), (0,)), ((), ())), preferred_element_type=F32)


def _split3(x):
    h1 = x.astype(BF16)
    r1 = x - h1.astype(F32)
    h2 = r1.astype(BF16)
    h3 = (r1 - h2.astype(F32)).astype(BF16)
    return h1, h2, h3


def _mod_index(bm):
    return (lambda b, *_: (b, 0, 0)) if bm > 1 else (lambda b, *_: (0, 0, 0))


def _mod_kernel(x_ref, w_ref, b_ref, o_ref):
    x = x_ref[...]
    s = x * _sigmoid(x)
    w = w_ref[0]
    sh, sm, _ = _split3(s)
    wh, wm, _ = _split3(w)
    o_ref[0] = _dot(sh, wh) + _dot(sh, wm) + _dot(sm, wh) + b_ref[0]


def _modulation(cc, mod_w, mod_b):
    depth, d, e = mod_w.shape
    tn = 1536
    return pl.pallas_call(
        _mod_kernel,
        grid=(depth, e // tn),
        in_specs=[pl.BlockSpec((8, d), lambda l, j: (0, 0)),
                  pl.BlockSpec((1, d, tn), lambda l, j: (l, 0, j)),
                  pl.BlockSpec((1, 1, tn), lambda l, j: (l, 0, j))],
        out_specs=pl.BlockSpec((1, 8, tn), lambda l, j: (l, 0, j)),
        out_shape=jax.ShapeDtypeStruct((depth, 8, e), F32),
        compiler_params=_params("parallel", "parallel"),
        name="modulation",
    )(cc, mod_w, mod_b.reshape(depth, 1, e))


def _ffn_kernel(x_ref, mod_ref, g_ref, w1_ref, w3_ref, w2_ref, o_ref, f_scr, acc_scr):
    j = pl.program_id(2)

    @pl.when(j == 0)
    def _():
        f_scr[...] = _norm_mod(x_ref[0], g_ref[...], mod_ref, 3).astype(BF16)
        acc_scr[...] = jnp.zeros_like(acc_scr)

    f = f_scr[...]
    h1 = _dot(f, w1_ref[...])
    h3 = _dot(f, w3_ref[...])
    hm = (h1 * _sigmoid(h1)) * h3
    acc_scr[...] += _dot(hm.astype(BF16), w2_ref[...])

    @pl.when(j == pl.num_programs(2) - 1)
    def _():
        o_ref[0] = x_ref[0] + mod_ref[0, 5:6, :] * acc_scr[...]


def _ffn(x, mod, g, w1, w3, w2, tm):
    b, n, d = x.shape
    f = w1.shape[1]
    tf = 256
    return pl.pallas_call(
        _ffn_kernel,
        grid=(b, n // tm, f // tf),
        in_specs=[pl.BlockSpec((1, tm, d), lambda b_, i, j: (b_, i, 0)),
                  pl.BlockSpec((1, 6, d), _mod_index(mod.shape[0])),
                  pl.BlockSpec((1, d), lambda b_, i, j: (0, 0)),
                  pl.BlockSpec((d, tf), lambda b_, i, j: (0, j)),
                  pl.BlockSpec((d, tf), lambda b_, i, j: (0, j)),
                  pl.BlockSpec((tf, d), lambda b_, i, j: (j, 0))],
        out_specs=pl.BlockSpec((1, tm, d), lambda b_, i, j: (b_, i, 0)),
        out_shape=jax.ShapeDtypeStruct((b, n, d), F32),
        scratch_shapes=[pltpu.VMEM((tm, d), BF16), pltpu.VMEM((tm, d), F32)],
        compiler_params=_params("parallel", "parallel", "arbitrary"),
        name="ffn",
    )(x, mod, g, w1, w3, w2)


def _oproj_kernel(x_ref, res_ref, mod_ref, w_ref, o_ref):
    o_ref[0] = res_ref[0] + mod_ref[0, 2:3, :] * _dot(x_ref[0], w_ref[...])


def _oproj(x, res, mod, w, tm):
    b, n, k = x.shape
    d = w.shape[1]
    return pl.pallas_call(
        _oproj_kernel,
        grid=(b, n // tm),
        in_specs=[pl.BlockSpec((1, tm, k), lambda b_, i: (b_, i, 0)),
                  pl.BlockSpec((1, tm, d), lambda b_, i: (b_, i, 0)),
                  pl.BlockSpec((1, 6, d), _mod_index(mod.shape[0])),
                  pl.BlockSpec((k, d), lambda b_, i: (0, 0))],
        out_specs=pl.BlockSpec((1, tm, d), lambda b_, i: (b_, i, 0)),
        out_shape=jax.ShapeDtypeStruct((b, n, d), F32),
        compiler_params=_params("parallel", "parallel"),
        name="oproj",
    )(x, res, mod, w)


def _gqa_qkv_kernel(x_ref, mod_ref, g_ref, w_ref, gq_ref, gk_ref, cos_ref, sin_ref,
                    q_ref, k_ref, v_ref, *, rope, n_q, n_kv, hd, scale):
    a = _norm_mod(x_ref[0], g_ref[...], mod_ref, 0).astype(BF16)
    qkv = _dot(a, w_ref[...])
    if rope:
        cos = cos_ref[...]
        sin = sin_ref[...]
        even = (lax.broadcasted_iota(jnp.int32, cos.shape, 1) % 2) == 0

    def head(xh, g):
        y = _rms(xh, RMS_EPS) * g
        if rope:
            partner = jnp.where(even, pltpu.roll(y, hd - 1, 1), pltpu.roll(y, 1, 1))
            y = y * cos + partner * sin
        return y

    gq = gq_ref[...]
    gk = gk_ref[...]
    for h in range(n_q):
        sl = slice(h * hd, (h + 1) * hd)
        q_ref[0, :, sl] = (head(qkv[:, sl], gq) * scale).astype(BF16)
    for h in range(n_kv):
        sl = slice(h * hd, (h + 1) * hd)
        k_ref[0, :, sl] = head(qkv[:, n_q * hd + h * hd:n_q * hd + (h + 1) * hd], gk).astype(BF16)
    v_ref[0] = qkv[:, (n_q + n_kv) * hd:].astype(BF16)


def _gqa_qkv(x, mod, g, w, gq, gk, cos, sin, tm, rope):
    b, n, d = x.shape
    hd = AT_HEAD
    n_kv = AT_KV_HEADS
    n_q = w.shape[1] // hd - 2 * n_kv
    kern = functools.partial(_gqa_qkv_kernel, rope=rope, n_q=n_q, n_kv=n_kv, hd=hd, scale=hd ** -0.5)
    tab = (lambda b_, i: (i, 0)) if rope else (lambda b_, i: (0, 0))
    return pl.pallas_call(
        kern,
        grid=(b, n // tm),
        in_specs=[pl.BlockSpec((1, tm, d), lambda b_, i: (b_, i, 0)),
                  pl.BlockSpec((1, 6, d), _mod_index(mod.shape[0])),
                  pl.BlockSpec((1, d), lambda b_, i: (0, 0)),
                  pl.BlockSpec(w.shape, lambda b_, i: (0, 0)),
                  pl.BlockSpec((1, hd), lambda b_, i: (0, 0)),
                  pl.BlockSpec((1, hd), lambda b_, i: (0, 0)),
                  pl.BlockSpec((tm, hd), tab),
                  pl.BlockSpec((tm, hd), tab)],
        out_specs=[pl.BlockSpec((1, tm, n_q * hd), lambda b_, i: (b_, i, 0)),
                   pl.BlockSpec((1, tm, n_kv * hd), lambda b_, i: (b_, i, 0)),
                   pl.BlockSpec((1, tm, n_kv * hd), lambda b_, i: (b_, i, 0))],
        out_shape=[jax.ShapeDtypeStruct((b, n, n_q * hd), BF16),
                   jax.ShapeDtypeStruct((b, n, n_kv * hd), BF16),
                   jax.ShapeDtypeStruct((b, n, n_kv * hd), BF16)],
        compiler_params=_params("parallel", "parallel"),
        name="gqa_qkv",
    )(x, mod, g, w, gq, gk, cos, sin)


def _flash_kernel(q_ref, k_ref, v_ref, o_ref, m_scr, l_scr, acc_scr, *, kvb, grp, hd):
    j = pl.program_id(3)

    @pl.when(j == 0)
    def _():
        m_scr[...] = jnp.full_like(m_scr, NEG_BIG)
        l_scr[...] = jnp.zeros_like(l_scr)
        acc_scr[...] = jnp.zeros_like(acc_scr)

    heads = [(kh, kh * grp + g) for kh in range(kvb) for g in range(grp)]
    s_l = [_dot_nt(k_ref[0, :, kh * hd:(kh + 1) * hd], q_ref[0, :, hq * hd:(hq + 1) * hd])
           for kh, hq in heads]
    p_l, alpha_l = [], []
    for (kh, hq), s in zip(heads, s_l):
        m_prev = m_scr[hq]
        m_new = jnp.maximum(m_prev, jnp.max(s, axis=0, keepdims=True))
        alpha = jnp.exp(m_prev - m_new)
        p = jnp.exp(s - m_new)
        l_scr[hq] = alpha * l_scr[hq] + jnp.sum(p, axis=0, keepdims=True)
        m_scr[hq] = m_new
        p_l.append(p.astype(BF16))
        alpha_l.append(alpha)
    for (kh, hq), p, alpha in zip(heads, p_l, alpha_l):
        pv = _dot_tn(v_ref[0, :, kh * hd:(kh + 1) * hd], p)
        acc_scr[hq] = alpha * acc_scr[hq] + pv

    @pl.when(j == pl.num_programs(3) - 1)
    def _():
        for hq in range(kvb * grp):
            o_ref[0, :, hq * hd:(hq + 1) * hd] = (acc_scr[hq] / l_scr[hq]).T.astype(BF16)


def _flash(q, k, v, hd, grp, kvb, tq, tk):
    b, nq, dq = q.shape
    nk, dk = k.shape[1], k.shape[2]
    nblk = dk // (kvb * hd)
    kern = functools.partial(_flash_kernel, kvb=kvb, grp=grp, hd=hd)
    nh = kvb * grp
    return pl.pallas_call(
        kern,
        grid=(b, nblk, nq // tq, nk // tk),
        in_specs=[pl.BlockSpec((1, tq, nh * hd), lambda b_, g, i, j: (b_, i, g)),
                  pl.BlockSpec((1, tk, kvb * hd), lambda b_, g, i, j: (b_, j, g)),
                  pl.BlockSpec((1, tk, kvb * hd), lambda b_, g, i, j: (b_, j, g))],
        out_specs=pl.BlockSpec((1, tq, nh * hd), lambda b_, g, i, j: (b_, i, g)),
        out_shape=jax.ShapeDtypeStruct((b, nq, dq), BF16),
        scratch_shapes=[pltpu.VMEM((nh, 1, tq), F32), pltpu.VMEM((nh, 1, tq), F32),
                        pltpu.VMEM((nh, hd, tq), F32)],
        compiler_params=_params("parallel", "parallel", "parallel", "arbitrary"),
        name="flash_attention",
    )(q, k, v)


def _na_qkv_kernel(x_ref, mod_ref, g_ref, w_ref, gq_ref, gk_ref, q_ref, k_ref, v_ref, *, nh, hd, scale):
    a = _norm_mod(x_ref[0], g_ref[...], mod_ref, 0).astype(BF16)
    qkv = _dot(a, w_ref[...])
    gq = gq_ref[...]
    gk = gk_ref[...]
    for h in range(nh):
        sl = slice(h * hd, (h + 1) * hd)
        q_ref[0, :, sl] = (_rms(qkv[:, sl], RMS_EPS) * gq * scale).astype(BF16)
        k_ref[0, :, sl] = (_rms(qkv[:, nh * hd + h * hd:nh * hd + (h + 1) * hd], RMS_EPS) * gk).astype(BF16)
    v_ref[0] = qkv[:, 2 * nh * hd:].astype(BF16)


def _na_qkv(x, mod, g, w, gq, gk, tm):
    b, n, d = x.shape
    hd = NA_HEAD
    nh = w.shape[1] // (3 * hd)
    kern = functools.partial(_na_qkv_kernel, nh=nh, hd=hd, scale=hd ** -0.5)
    spec_o = pl.BlockSpec((1, tm, nh * hd), lambda b_, i: (b_, i, 0))
    return pl.pallas_call(
        kern,
        grid=(b, n // tm),
        in_specs=[pl.BlockSpec((1, tm, d), lambda b_, i: (b_, i, 0)),
                  pl.BlockSpec((1, 6, d), _mod_index(mod.shape[0])),
                  pl.BlockSpec((1, d), lambda b_, i: (0, 0)),
                  pl.BlockSpec(w.shape, lambda b_, i: (0, 0)),
                  pl.BlockSpec((1, hd), lambda b_, i: (0, 0)),
                  pl.BlockSpec((1, hd), lambda b_, i: (0, 0))],
        out_specs=[spec_o, spec_o, spec_o],
        out_shape=[jax.ShapeDtypeStruct((b, n, nh * hd), BF16)] * 3,
        compiler_params=_params("parallel", "parallel"),
        name="na_qkv",
    )(x, mod, g, w, gq, gk)


def _na_kernel(cls_ref, q_ref, k_ref, v_ref, kc_ref, vc_ref, bias_ref, o_ref, *, hb, hd, width, kh, rows):
    del cls_ref
    r = pl.program_id(2)
    rs = jnp.clip(r - kh // 2, 0, rows - kh)
    start = pl.multiple_of(rs * width, width)
    win = pl.ds(start, kh * width)
    sls = [slice(h * hd, (h + 1) * hd) for h in range(hb)]
    s_l = [_dot_nt(q_ref[0, :, sl], k_ref[0, win, sl]) for sl in sls]
    sc_l = [_dot_nt(q_ref[0, :, sl], kc_ref[0, :, sl]) for sl in sls]
    p_l, pc_l, l_l = [], [], []
    for h in range(hb):
        s = s_l[h] + bias_ref[0, h]
        sc = sc_l[h]
        m = jnp.maximum(jnp.max(s, axis=-1, keepdims=True), jnp.max(sc, axis=-1, keepdims=True))
        p = jnp.exp(s - m)
        pc = jnp.exp(sc - m)
        l_l.append(jnp.sum(p, axis=-1, keepdims=True) + jnp.sum(pc, axis=-1, keepdims=True))
        p_l.append(p.astype(BF16))
        pc_l.append(pc.astype(BF16))
    o_l = [_dot(p_l[h], v_ref[0, win, sls[h]]) + _dot(pc_l[h], vc_ref[0, :, sls[h]]) for h in range(hb)]
    for h in range(hb):
        o_ref[0, :, sls[h]] = (o_l[h] / l_l[h]).astype(BF16)


def _na_tables(rpb, rows):
    kh, kw = min(WIN_H, rows), WIN_W
    r_idx = np.arange(rows)
    delta = np.clip(r_idx - kh // 2, 0, rows - kh) - r_idx
    classes, cls_of_row = np.unique(delta, return_inverse=True)
    c_idx = np.arange(GRID_W)
    cs = np.clip(c_idx - kw // 2, 0, GRID_W - kw)
    kcol = np.arange(GRID_W)
    valid = (kcol[None, :] >= cs[:, None]) & (kcol[None, :] < cs[:, None] + kw)
    pad = GRID_W
    rp = jnp.pad(rpb.astype(F32), ((0, 0), (0, 0), (pad, pad)))
    toep = jnp.stack([rp[:, :, pad + WIN_W - 1 - c:pad + WIN_W - 1 - c + GRID_W] for c in range(GRID_W)], axis=2)
    toep = jnp.where(valid[None, None, :, :], toep, NEG_BIG)
    tab = jnp.stack([toep[:, int(dl) + WIN_H - 1:int(dl) + WIN_H - 1 + kh] for dl in classes], axis=0)
    tab = tab.transpose(0, 1, 3, 2, 4).reshape(len(classes), rpb.shape[0], GRID_W, kh * GRID_W)
    return tab, jnp.asarray(cls_of_row, jnp.int32), kh


def _na_attention(q, k, v, kc, vc, rpb):
    b, t, d = q.shape
    c = kc.shape[1]
    hd = NA_HEAD
    hb = d // hd
    rows = t // GRID_W
    tab, cls_of_row, kh = _na_tables(rpb, rows)
    kern = functools.partial(_na_kernel, hb=hb, hd=hd, width=GRID_W, kh=kh, rows=rows)
    grid_spec = pltpu.PrefetchScalarGridSpec(
        num_scalar_prefetch=1,
        grid=(b, d // (hb * hd), rows),
        in_specs=[pl.BlockSpec((1, GRID_W, hb * hd), lambda b_, g, r, cls: (b_, r, g)),
                  pl.BlockSpec((1, t, hb * hd), lambda b_, g, r, cls: (b_, 0, g)),
                  pl.BlockSpec((1, t, hb * hd), lambda b_, g, r, cls: (b_, 0, g)),
                  pl.BlockSpec((1, c, hb * hd), lambda b_, g, r, cls: (b_, 0, g)),
                  pl.BlockSpec((1, c, hb * hd), lambda b_, g, r, cls: (b_, 0, g)),
                  pl.BlockSpec((1, hb, GRID_W, kh * GRID_W), lambda b_, g, r, cls: (cls[r], g, 0, 0))],
        out_specs=pl.BlockSpec((1, GRID_W, hb * hd), lambda b_, g, r, cls: (b_, r, g)),
    )
    return pl.pallas_call(
        kern,
        grid_spec=grid_spec,
        out_shape=jax.ShapeDtypeStruct((b, t, d), BF16),
        compiler_params=_params("parallel", "parallel", "arbitrary"),
        name="na_attention",
    )(cls_of_row, q, k, v, kc, vc, tab)


def _rw_feat_kernel(x_ref, xp_ref, xn_ref, mod_ref, g_ref, mu_ref, wr_ref, wk_ref, wv_ref,
                    w1_ref, w2_ref, a1_ref, a2_ref, g1_ref, g2_ref, w0_ref, a0_ref,
                    r_ref, k_ref, v_ref, lw_ref, a_ref, gate_ref, *, tm, lora_w, lora_a):
    i = pl.program_id(1)
    g = g_ref[...]
    a = _norm_mod(x_ref[0], g, mod_ref, 0)
    a_prev = _norm_mod(xp_ref[0], g, mod_ref, 0)[7:8, :]
    a_next = _norm_mod(xn_ref[0], g, mod_ref, 0)[0:1, :]
    a_prev = jnp.where(i == 0, 0.0, a_prev)
    a_next = jnp.where(i == pl.num_programs(1) - 1, 0.0, a_next)
    row = lax.broadcasted_iota(jnp.int32, (tm, 1), 0)
    prev = jnp.where(row == 0, a_prev, pltpu.roll(a, 1, 0))
    nxt = jnp.where(row == tm - 1, a_next, pltpu.roll(a, tm - 1, 0))
    xx = 0.5 * (prev + nxt) - a

    def mix(j):
        return (a + xx * mu_ref[j:j + 1, :]).astype(BF16)

    r_ref[0] = _dot(mix(0), wr_ref[...])
    k_ref[0] = _dot(mix(2), wk_ref[...])
    v_ref[0] = _dot(mix(3), wv_ref[...])
    tw = jnp.tanh(_dot(mix(1), w1_ref[...]))
    ta = _dot(mix(4), a1_ref[...])
    for e in range(2):
        lora = _dot(tw[:, e * lora_w:(e + 1) * lora_w].astype(BF16), w2_ref[e])
        lw_ref[e, 0] = -EXP_M05 * _sigmoid(w0_ref[e:e + 1, :] + lora)
        la = _dot(ta[:, e * lora_a:(e + 1) * lora_a].astype(BF16), a2_ref[e])
        a_ref[e, 0] = _sigmoid(a0_ref[e:e + 1, :] + la)
    gate_ref[0] = _dot(_sigmoid(_dot(mix(5), g1_ref[...])).astype(BF16), g2_ref[...])


def _rw_features(x, mod, g, p, tm):
    b, n, d = x.shape
    nb8 = n // 8
    tb = tm // 8
    lora_w = p["w2"].shape[1]
    lora_a = p["a2"].shape[1]
    kern = functools.partial(_rw_feat_kernel, tm=tm, lora_w=lora_w, lora_a=lora_a)
    full = lambda arr: pl.BlockSpec(arr.shape, lambda b_, i: (0,) * arr.ndim)
    tile = pl.BlockSpec((1, tm, d), lambda b_, i: (b_, i, 0))
    tile2 = pl.BlockSpec((2, 1, tm, d), lambda b_, i: (0, b_, i, 0))
    names = ("mu", "wr", "wk", "wv", "w1", "w2", "a1", "a2", "g1", "g2", "w0", "a0")
    return pl.pallas_call(
        kern,
        grid=(b, n // tm),
        in_specs=[tile,
                  pl.BlockSpec((1, 8, d), lambda b_, i: (b_, jnp.maximum(i * tb - 1, 0), 0)),
                  pl.BlockSpec((1, 8, d), lambda b_, i: (b_, jnp.minimum((i + 1) * tb, nb8 - 1), 0)),
                  pl.BlockSpec((1, 6, d), _mod_index(mod.shape[0])),
                  pl.BlockSpec((1, d), lambda b_, i: (0, 0))] + [full(p[nm]) for nm in names],
        out_specs=[tile, tile, tile, tile2, tile2, tile],
        out_shape=[jax.ShapeDtypeStruct((b, n, d), F32)] * 3
        + [jax.ShapeDtypeStruct((2, b, n, d), F32)] * 2 + [jax.ShapeDtypeStruct((b, n, d), F32)],
        compiler_params=_params("parallel", "parallel"),
        name="rwkv_features",
    )(x, x, x, mod, g, *[p[nm] for nm in names])


def _rw_scan_kernel(s0_ref, rf_ref, kf_ref, vf_ref, rr_ref, kr_ref, vr_ref, lwf_ref, lwr_ref, af_ref, ar_ref,
                    kkp_ref, kap_ref, rkp_ref, yf_ref, yr_ref, bonf_ref, bonr_ref, sf_ref, s_scr, *, L, H, N):
    c = pl.program_id(1)

    @pl.when(c == 0)
    def _():
        s_scr[...] = s0_ref[:, 0]

    D = H * N
    ri = lax.broadcasted_iota(jnp.int32, (L, L), 0)
    ci = lax.broadcasted_iota(jnp.int32, (L, L), 1)
    eye = jnp.where(ri == ci, 1.0, 0.0).astype(F32)
    ri2 = lax.broadcasted_iota(jnp.int32, (2 * L, 2 * L), 0)
    ci2 = lax.broadcasted_iota(jnp.int32, (2 * L, 2 * L), 1)
    rr = ri2 & (L - 1)
    cc = ci2 & (L - 1)
    strict_rows = jnp.where(ri2 < L, 1, 0)
    ind = jnp.where(lax.broadcasted_iota(jnp.int32, (D, 128), 0) // N
                    == lax.broadcasted_iota(jnp.int32, (D, 128), 1), 1.0, 0.0).astype(BF16)
    n_fact = int(math.log2(L))
    sls = [slice(h * N, (h + 1) * N) for h in range(H)]

    dirs = []
    for rev, (r_ref, k_ref, v_ref, lw_ref, a_ref) in enumerate(
            ((rf_ref, kf_ref, vf_ref, lwf_ref, af_ref), (rr_ref, kr_ref, vr_ref, lwr_ref, ar_ref))):
        dist = (ci - ri) if rev else (ri - ci)
        incl_b = jnp.where(dist >= 0, 1.0, 0.0).astype(BF16)
        mask2 = ((cc - rr) if rev else (rr - cc)) >= strict_rows
        lw = lw_ref[0, 0]
        l1, l2, l3 = _split3(lw)
        cw = _dot(incl_b, l1) + _dot(incl_b, l2) + _dot(incl_b, l3)
        tot = cw[0:1, :] if rev else cw[L - 1:L, :]
        a = a_ref[0, 0]
        k = k_ref[0]
        r = r_ref[0]
        v = v_ref[0]
        kk_raw = k * kkp_ref[...]
        kd = k * (1.0 + (a - 1.0) * kap_ref[...])
        t1, t2, t3 = _split3(jnp.concatenate([kk_raw * kk_raw, r * kd * rkp_ref[...]], axis=0))
        hs = _dot(t1, ind) + _dot(t2, ind) + _dot(t3, ind)
        dirs.append(dict(
            mask2=mask2, a=a, v=v, kk_raw=kk_raw,
            inv_norm=jnp.minimum(lax.rsqrt(hs[:L]), 1.0 / NORM_EPS), rk_sum=hs[L:],
            e_prev=jnp.exp(cw - lw), e_neg=jnp.exp(-cw), e_rem=jnp.exp(tot - cw), e_tot=jnp.exp(tot),
            r_t=r * jnp.exp(cw), k_t=kd * jnp.exp(-cw), k_h=kd * jnp.exp(tot - cw)))

    items = [(e, h) for e in range(2) for h in range(H)]
    x_ar, z_bk, b_l, vb = [], [], [], []
    for e, h in items:
        d, sl = dirs[e], sls[h]
        kk = d["kk_raw"][:, sl] * d["inv_norm"][:, h:h + 1]
        b = kk * d["a"][:, sl]
        b_l.append(b)
        x_ar.append(jnp.concatenate([-kk * d["e_prev"][:, sl], d["r_t"][:, sl]], axis=0).astype(BF16))
        z_bk.append(jnp.concatenate([b * d["e_neg"][:, sl], d["k_t"][:, sl]], axis=0).astype(BF16))
        vb.append(d["v"][:, sl].astype(BF16))
    idx = range(len(items))
    gram = [jnp.where(dirs[items[i][0]]["mask2"], _dot_nt(x_ar[i], z_bk[i]), 0.0) for i in idx]
    s_old = [s_scr[e, h] for e, h in items]
    xs = [_dot_nt(x_ar[i], s_old[i].astype(BF16)) for i in idx]
    rhs = [xs[i][:L] + _dot(gram[i][:L, L:].astype(BF16), vb[i]) for i in idx]

    qb = [gram[i][:L, :L].astype(BF16) for i in idx]
    t_inv = [eye + gram[i][:L, :L] for i in idx]
    q = [_dot(qb[i], qb[i]) for i in idx]
    for _ in range(n_fact - 2):
        qb = [q[i].astype(BF16) for i in idx]
        st = [_dot(jnp.concatenate([t_inv[i].astype(BF16), qb[i]], axis=0), qb[i]) for i in idx]
        t_inv = [t_inv[i] + st[i][:L] for i in idx]
        q = [st[i][L:] for i in idx]
    st = [_dot(t_inv[i].astype(BF16), q[i].astype(BF16)) for i in idx]
    t_inv = [t_inv[i] + st[i] for i in idx]

    u = [_dot(t_inv[i].astype(BF16), rhs[i].astype(BF16)) for i in idx]
    uv = [jnp.concatenate([u[i].astype(BF16), vb[i]], axis=0) for i in idx]
    y = [xs[i][L:] + _dot(gram[i][L:, :].astype(BF16), uv[i]) for i in idx]
    bk = [jnp.concatenate([b_l[i] * dirs[e]["e_rem"][:, sls[h]], dirs[e]["k_h"][:, sls[h]]], axis=0).astype(BF16)
          for i, (e, h) in enumerate(items)]
    s_new = [s_old[i] * dirs[e]["e_tot"][:, sls[h]] + _dot_tn(uv[i], bk[i]) for i, (e, h) in enumerate(items)]
    for i, (e, h) in enumerate(items):
        sl = sls[h]
        y_ref, bon_ref = (yr_ref, bonr_ref) if e else (yf_ref, bonf_ref)
        y_ref[0, :, sl] = y[i]
        s_scr[e, h] = s_new[i]
        bon_ref[0, :, sl] = dirs[e]["rk_sum"][:, h:h + 1] * dirs[e]["v"][:, sl]

    @pl.when(c == pl.num_programs(1) - 1)
    def _():
        sf_ref[:, 0] = s_scr[...]


def _rw_scan(s0, r, k, v, lw, a, kkp, kap, rkp):
    b, n, d = r.shape
    L = SCAN_CHUNK
    N = RW_HEAD
    H = d // N
    nc = n // L
    kern = functools.partial(_rw_scan_kernel, L=L, H=H, N=N)
    tok_f = pl.BlockSpec((1, L, d), lambda b_, c: (b_, c, 0))
    tok_r = pl.BlockSpec((1, L, d), lambda b_, c: (b_, nc - 1 - c, 0))
    dir_f = pl.BlockSpec((1, 1, L, d), lambda b_, c: (0, b_, c, 0))
    dir_r = pl.BlockSpec((1, 1, L, d), lambda b_, c: (1, b_, nc - 1 - c, 0))
    vec = pl.BlockSpec((1, d), lambda b_, c: (0, 0))
    st = pl.BlockSpec((2, 1, H, N, N), lambda b_, c: (0, b_, 0, 0, 0))
    tok_shape = jax.ShapeDtypeStruct((b, n, d), F32)
    return pl.pallas_call(
        kern,
        grid=(b, nc),
        in_specs=[st, tok_f, tok_f, tok_f, tok_r, tok_r, tok_r, dir_f, dir_r, dir_f, dir_r, vec, vec, vec],
        out_specs=[tok_f, tok_r, tok_f, tok_r, st],
        out_shape=[tok_shape, tok_shape, tok_shape, tok_shape, jax.ShapeDtypeStruct((2, b, H, N, N), F32)],
        scratch_shapes=[pltpu.VMEM((2, H, N, N), F32)],
        compiler_params=_params("parallel", "arbitrary"),
        name="rwkv_scan",
    )(s0, r, k, v, r, k, v, lw, lw, a, a, kkp, kap, rkp)


def _rw_out_kernel(yf_ref, yr_ref, bonf_ref, bonr_ref, gate_ref, res_ref, mod_ref, lg_ref, lb_ref, wo_ref, o_ref,
                   yn_scr, *, H, N):
    y = yf_ref[0] + yr_ref[0]
    for h in range(H):
        sl = slice(h * N, (h + 1) * N)
        yh = y[:, sl]
        dlt = yh - jnp.mean(yh, axis=-1, keepdims=True)
        var = jnp.mean(dlt * dlt, axis=-1, keepdims=True)
        yn_scr[:, sl] = dlt * lax.rsqrt(var + LNX_EPS)
    z = (yn_scr[...] * lg_ref[...] + lb_ref[...] + (bonf_ref[0] + bonr_ref[0])) * gate_ref[0]
    o_ref[0] = res_ref[0] + mod_ref[0, 2:3, :] * _dot(z.astype(BF16), wo_ref[...])


def _rw_readout(ys, bons, gate, res, mod, lg, lb, wo, tm):
    b, n, d = res.shape
    N = RW_HEAD
    kern = functools.partial(_rw_out_kernel, H=d // N, N=N)
    tile = pl.BlockSpec((1, tm, d), lambda b_, i: (b_, i, 0))
    vec = pl.BlockSpec((1, d), lambda b_, i: (0, 0))
    return pl.pallas_call(
        kern,
        grid=(b, n // tm),
        in_specs=[tile, tile, tile, tile, tile, tile, pl.BlockSpec((1, 6, d), _mod_index(mod.shape[0])),
                  vec, vec, pl.BlockSpec((d, d), lambda b_, i: (0, 0))],
        out_specs=tile,
        out_shape=jax.ShapeDtypeStruct((b, n, d), F32),
        scratch_shapes=[pltpu.VMEM((tm, d), F32)],
        compiler_params=_params("parallel", "parallel"),
        name="rwkv_readout",
    )(*ys, *bons, gate, res, mod, lg, lb, wo)


def _rope_tables(n_tok, hd):
    t = jnp.arange(n_tok)
    rows = (t // GRID_W).astype(F32)
    cols = (t % GRID_W).astype(F32)
    d_axis = hd // 2
    inv = jnp.float32(ROPE_BASE) ** (-jnp.arange(0, d_axis, 2, dtype=F32) / d_axis)
    ang = jnp.concatenate([rows[:, None] * inv, cols[:, None] * inv], axis=-1)
    cos = jnp.repeat(jnp.cos(ang), 2, axis=-1)
    sign = jnp.tile(jnp.asarray([-1.0, 1.0], F32), hd // 2)
    sin = jnp.repeat(jnp.sin(ang), 2, axis=-1) * sign
    return cos, sin


def _tiles(n):
    return min(n, 512), min(n, 1024)


def kernel(x, c, ctx, c_ctx, mod_w, mod_b, norm_mix, norm_ffn, ff_w1, ff_w3, ff_w2, rw_mu, rw_wr, rw_wk, rw_wv, rw_wo, rw_w0, rw_w1, rw_w2, rw_a0, rw_a1, rw_a2, rw_g1, rw_g2, rw_kk, rw_ka, rw_rk, rw_lnx_g, rw_lnx_b, at_wq, at_wk, at_wv, at_wo, at_gq, at_gk, na_wqkv, na_wo, na_gq, na_gk, na_rpb):
    B, T, D = x.shape
    C = ctx.shape[1]
    depth = mod_w.shape[0]
    bf = lambda w: w.astype(BF16)

    cc = jnp.concatenate([c, c_ctx[None, :], jnp.zeros((8 - B - 1, D), F32)], axis=0)
    mods = _modulation(cc, mod_w, mod_b)
    mod_lat = mods[:, :B].reshape(depth, B, 6, D)
    mod_ctx = mods[:, B:B + 1].reshape(depth, 1, 6, D)

    tl, tl_ffn = _tiles(T)
    tc, tc_ffn = _tiles(C)
    h_lat, h_ctx = x, ctx
    for i in range(depth):
        need_ctx = i < depth - 1
        kind, j = i % 3, i // 3
        ml, mc = mod_lat[i], mod_ctx[i]
        g_mix = norm_mix[i][None, :]
        if kind == 0:
            cat = lambda w: jnp.concatenate([w[0], w[1]], axis=1)
            p = dict(mu=rw_mu[j], wr=bf(rw_wr[j]), wk=bf(rw_wk[j]), wv=bf(rw_wv[j]),
                     w1=bf(cat(rw_w1[j])), w2=bf(rw_w2[j]), a1=bf(cat(rw_a1[j])), a2=bf(rw_a2[j]),
                     g1=bf(rw_g1[j]), g2=bf(rw_g2[j]), w0=rw_w0[j], a0=rw_a0[j])
            kkp, kap, rkp = rw_kk[j][None, :], rw_ka[j][None, :], rw_rk[j].reshape(1, D)
            lg, lb, wo = rw_lnx_g[j][None, :], rw_lnx_b[j][None, :], bf(rw_wo[j])
            r_c, k_c, v_c, lw_c, a_c, gate_c = _rw_features(h_ctx, mc, g_mix, p, min(C, 256))
            r_l, k_l, v_l, lw_l, a_l, gate_l = _rw_features(h_lat, ml, g_mix, p, min(T, 256))
            s0 = jnp.zeros((2, B, D // RW_HEAD, RW_HEAD, RW_HEAD), F32)
            *out_c, s_c = _rw_scan(s0, r_c, k_c, v_c, lw_c, a_c, kkp, kap, rkp)
            *out_l, _ = _rw_scan(s_c, r_l, k_l, v_l, lw_l, a_l, kkp, kap, rkp)
            h_lat = _rw_readout(out_l[:2], out_l[2:], gate_l, h_lat, ml, lg, lb, wo, min(T, 256))
            if need_ctx:
                h_ctx = _rw_readout(out_c[:2], out_c[2:], gate_c, h_ctx, mc, lg, lb, wo, min(C, 256))
        elif kind == 1:
            w = bf(jnp.concatenate([at_wq[j], at_wk[j], at_wv[j]], axis=1))
            gq, gk, wo = at_gq[j][None, :], at_gk[j][None, :], bf(at_wo[j])
            cos, sin = _rope_tables(T, AT_HEAD)
            q_l, k_l, v_l = _gqa_qkv(h_lat, ml, g_mix, w, gq, gk, cos, sin, tl, True)
            q_c, k_c, v_c = _gqa_qkv(h_ctx, mc, g_mix, w, gq, gk, cos, sin, tc, False)
            k_all = jnp.concatenate([k_l, k_c], axis=1)
            v_all = jnp.concatenate([v_l, v_c], axis=1)
            grp = q_l.shape[2] // k_l.shape[2]
            o_l = _flash(q_l, k_all, v_all, AT_HEAD, grp, 1, tl, math.gcd(T + C, 256))
            h_lat = _oproj(o_l, h_lat, ml, wo, tl)
            if need_ctx:
                o_c = _flash(q_c, k_c, v_c, AT_HEAD, grp, 1, tc, tc)
                h_ctx = _oproj(o_c, h_ctx, mc, wo, tc)
        else:
            w, wo = bf(na_wqkv[j]), bf(na_wo[j])
            gq, gk = na_gq[j][None, :], na_gk[j][None, :]
            q_l, k_l, v_l = _na_qkv(h_lat, ml, g_mix, w, gq, gk, min(T, 256))
            q_c, k_c, v_c = _na_qkv(h_ctx, mc, g_mix, w, gq, gk, min(C, 256))
            o_l = _na_attention(q_l, k_l, v_l, k_c, v_c, na_rpb[j])
            h_lat = _oproj(o_l, h_lat, ml, wo, tl)
            if need_ctx:
                o_c = _flash(q_c, k_c, v_c, NA_HEAD, 1, 2, tc, tc)
                h_ctx = _oproj(o_c, h_ctx, mc, wo, tc)
        g_ffn = norm_ffn[i][None, :]
        w1, w3, w2 = bf(ff_w1[i]), bf(ff_w3[i]), bf(ff_w2[i])
        h_lat = _ffn(h_lat, ml, g_ffn, w1, w3, w2, tl_ffn)
        if need_ctx:
            h_ctx = _ffn(h_ctx, mc, g_ffn, w1, w3, w2, tc_ffn)
    return h_lat
```

```python
import functools
import math

import numpy as np
import jax
import jax.numpy as jnp
from jax import lax
from jax.experimental import pallas as pl
from jax.experimental.pallas import tpu as pltpu

F32 = jnp.float32
BF16 = jnp.bfloat16

GRID_W = 64
RW_HEAD = 64
AT_HEAD = 128
AT_KV_HEADS = 2
NA_HEAD = 64
WIN_H = 8
WIN_W = 16
ROPE_BASE = 10000.0
RMS_EPS = 1e-6
LNX_EPS = 64e-5
NORM_EPS = 1e-12
NEG_BIG = -1e30
EXP_M05 = math.exp(-0.5)
LOG2_E = math.log2(math.e)

SCAN_CHUNK = 64
VMEM_LIMIT_BYTES_V7X = 48 * 1024 * 1024


def _params(*sem):
    return pltpu.CompilerParams(dimension_semantics=sem, vmem_limit_bytes=VMEM_LIMIT_BYTES_V7X)


def _sigmoid(x):
    return 1.0 / (1.0 + jnp.exp(-x))


def _rms(x, eps):
    return x * lax.rsqrt(jnp.mean(x * x, axis=-1, keepdims=True) + eps)


def _norm_mod(x, g, mod_ref, k):
    return (_rms(x, RMS_EPS) * g) * (1.0 + mod_ref[0, k + 1:k + 2, :]) + mod_ref[0, k:k + 1, :]


def _dot(a, b):
    return jnp.dot(a, b, preferred_element_type=F32)


def _dot_nt(a, b):
    return lax.dot_general(a, b, (((1,), (1,)), ((), ())), preferred_element_type=F32)


def _dot_tn(a, b):
    return lax.dot_general(a, b, (((0,), (0,)), ((), ())), preferred_element_type=F32)


def _split3(x):
    h1 = x.astype(BF16)
    r1 = x - h1.astype(F32)
    h2 = r1.astype(BF16)
    h3 = (r1 - h2.astype(F32)).astype(BF16)
    return h1, h2, h3


def _dot3(x, m01):
    x1, x2, x3 = _split3(x)
    return _dot(x1, m01) + _dot(x2, m01) + _dot(x3, m01)


def _head_indicators(d, n):
    ind = jnp.where(lax.broadcasted_iota(jnp.int32, (d, 128), 0) // n
                    == lax.broadcasted_iota(jnp.int32, (d, 128), 1), 1.0, 0.0).astype(BF16)
    ind_t = jnp.where(lax.broadcasted_iota(jnp.int32, (128, d), 1) // n
                      == lax.broadcasted_iota(jnp.int32, (128, d), 0), 1.0, 0.0).astype(BF16)
    return ind, ind_t


def _mod_index(bm):
    return (lambda b, *_: (b, 0, 0)) if bm > 1 else (lambda b, *_: (0, 0, 0))


def _mod_kernel(x_ref, w_ref, b_ref, o_ref):
    x = x_ref[...]
    s = x * _sigmoid(x)
    w = w_ref[0]
    sh, sm, _ = _split3(s)
    wh, wm, _ = _split3(w)
    o_ref[0] = _dot(sh, wh) + _dot(sh, wm) + _dot(sm, wh) + b_ref[0]


def _modulation(cc, mod_w, mod_b):
    depth, d, e = mod_w.shape
    tn = 1536
    return pl.pallas_call(
        _mod_kernel,
        grid=(depth, e // tn),
        in_specs=[pl.BlockSpec((8, d), lambda l, j: (0, 0)),
                  pl.BlockSpec((1, d, tn), lambda l, j: (l, 0, j)),
                  pl.BlockSpec((1, 1, tn), lambda l, j: (l, 0, j))],
        out_specs=pl.BlockSpec((1, 8, tn), lambda l, j: (l, 0, j)),
        out_shape=jax.ShapeDtypeStruct((depth, 8, e), F32),
        compiler_params=_params("parallel", "parallel"),
        name="modulation",
    )(cc, mod_w, mod_b.reshape(depth, 1, e))


def _ffn_kernel(x_ref, mod_ref, g_ref, w1_ref, w3_ref, w2_ref, o_ref, f_scr, acc_scr):
    j = pl.program_id(2)

    @pl.when(j == 0)
    def _():
        f_scr[...] = _norm_mod(x_ref[0], g_ref[...], mod_ref, 3).astype(BF16)
        acc_scr[...] = jnp.zeros_like(acc_scr)

    f = f_scr[...]
    h1 = _dot(f, w1_ref[...])
    h3 = _dot(f, w3_ref[...])
    hm = (h1 * _sigmoid(h1)) * h3
    acc_scr[...] += _dot(hm.astype(BF16), w2_ref[...])

    @pl.when(j == pl.num_programs(2) - 1)
    def _():
        o_ref[0] = x_ref[0] + mod_ref[0, 5:6, :] * acc_scr[...]


def _ffn(x, mod, g, w1, w3, w2, tm):
    b, n, d = x.shape
    f = w1.shape[1]
    tf = 256
    return pl.pallas_call(
        _ffn_kernel,
        grid=(b, n // tm, f // tf),
        in_specs=[pl.BlockSpec((1, tm, d), lambda b_, i, j: (b_, i, 0)),
                  pl.BlockSpec((1, 6, d), _mod_index(mod.shape[0])),
                  pl.BlockSpec((1, d), lambda b_, i, j: (0, 0)),
                  pl.BlockSpec((d, tf), lambda b_, i, j: (0, j)),
                  pl.BlockSpec((d, tf), lambda b_, i, j: (0, j)),
                  pl.BlockSpec((tf, d), lambda b_, i, j: (j, 0))],
        out_specs=pl.BlockSpec((1, tm, d), lambda b_, i, j: (b_, i, 0)),
        out_shape=jax.ShapeDtypeStruct((b, n, d), F32),
        scratch_shapes=[pltpu.VMEM((tm, d), BF16), pltpu.VMEM((tm, d), F32)],
        compiler_params=_params("parallel", "parallel", "arbitrary"),
        name="ffn",
    )(x, mod, g, w1, w3, w2)


def _oproj_kernel(x_ref, res_ref, mod_ref, w_ref, o_ref):
    o_ref[0] = res_ref[0] + mod_ref[0, 2:3, :] * _dot(x_ref[0], w_ref[...])


def _oproj(x, res, mod, w, tm):
    b, n, k = x.shape
    d = w.shape[1]
    return pl.pallas_call(
        _oproj_kernel,
        grid=(b, n // tm),
        in_specs=[pl.BlockSpec((1, tm, k), lambda b_, i: (b_, i, 0)),
                  pl.BlockSpec((1, tm, d), lambda b_, i: (b_, i, 0)),
                  pl.BlockSpec((1, 6, d), _mod_index(mod.shape[0])),
                  pl.BlockSpec((k, d), lambda b_, i: (0, 0))],
        out_specs=pl.BlockSpec((1, tm, d), lambda b_, i: (b_, i, 0)),
        out_shape=jax.ShapeDtypeStruct((b, n, d), F32),
        compiler_params=_params("parallel", "parallel"),
        name="oproj",
    )(x, res, mod, w)


def _gqa_qkv_kernel(x_ref, mod_ref, g_ref, w_ref, gq_ref, gk_ref, cos_ref, sin_ref,
                    q_ref, k_ref, v_ref, *, rope, n_q, n_kv, hd, scale):
    a = _norm_mod(x_ref[0], g_ref[...], mod_ref, 0).astype(BF16)
    qkv = _dot(a, w_ref[...])
    if rope:
        cos = cos_ref[...]
        sin = sin_ref[...]
        even = (lax.broadcasted_iota(jnp.int32, cos.shape, 1) % 2) == 0

    def head(xh, g):
        y = _rms(xh, RMS_EPS) * g
        if rope:
            partner = jnp.where(even, pltpu.roll(y, hd - 1, 1), pltpu.roll(y, 1, 1))
            y = y * cos + partner * sin
        return y

    gq = gq_ref[...]
    gk = gk_ref[...]
    for h in range(n_q):
        sl = slice(h * hd, (h + 1) * hd)
        q_ref[0, :, sl] = (head(qkv[:, sl], gq) * scale).astype(BF16)
    for h in range(n_kv):
        sl = slice(h * hd, (h + 1) * hd)
        k_ref[0, :, sl] = head(qkv[:, n_q * hd + h * hd:n_q * hd + (h + 1) * hd], gk).astype(BF16)
    v_ref[0] = qkv[:, (n_q + n_kv) * hd:].astype(BF16)


def _gqa_qkv(x, mod, g, w, gq, gk, cos, sin, tm, rope):
    b, n, d = x.shape
    hd = AT_HEAD
    n_kv = AT_KV_HEADS
    n_q = w.shape[1] // hd - 2 * n_kv
    kern = functools.partial(_gqa_qkv_kernel, rope=rope, n_q=n_q, n_kv=n_kv, hd=hd, scale=hd ** -0.5 * LOG2_E)
    tab = (lambda b_, i: (i, 0)) if rope else (lambda b_, i: (0, 0))
    return pl.pallas_call(
        kern,
        grid=(b, n // tm),
        in_specs=[pl.BlockSpec((1, tm, d), lambda b_, i: (b_, i, 0)),
                  pl.BlockSpec((1, 6, d), _mod_index(mod.shape[0])),
                  pl.BlockSpec((1, d), lambda b_, i: (0, 0)),
                  pl.BlockSpec(w.shape, lambda b_, i: (0, 0)),
                  pl.BlockSpec((1, hd), lambda b_, i: (0, 0)),
                  pl.BlockSpec((1, hd), lambda b_, i: (0, 0)),
                  pl.BlockSpec((tm, hd), tab),
                  pl.BlockSpec((tm, hd), tab)],
        out_specs=[pl.BlockSpec((1, tm, n_q * hd), lambda b_, i: (b_, i, 0)),
                   pl.BlockSpec((1, tm, n_kv * hd), lambda b_, i: (b_, i, 0)),
                   pl.BlockSpec((1, tm, n_kv * hd), lambda b_, i: (b_, i, 0))],
        out_shape=[jax.ShapeDtypeStruct((b, n, n_q * hd), BF16),
                   jax.ShapeDtypeStruct((b, n, n_kv * hd), BF16),
                   jax.ShapeDtypeStruct((b, n, n_kv * hd), BF16)],
        compiler_params=_params("parallel", "parallel"),
        name="gqa_qkv",
    )(x, mod, g, w, gq, gk, cos, sin)


def _flash_kernel(q_ref, k_ref, v_ref, o_ref, m_scr, l_scr, acc_scr, *, kvb, grp, hd):
    j = pl.program_id(3)

    @pl.when(j == 0)
    def _():
        m_scr[...] = jnp.full_like(m_scr, NEG_BIG)
        l_scr[...] = jnp.zeros_like(l_scr)
        acc_scr[...] = jnp.zeros_like(acc_scr)

    heads = [(kh, kh * grp + g) for kh in range(kvb) for g in range(grp)]
    s_l = [_dot_nt(k_ref[0, :, kh * hd:(kh + 1) * hd], q_ref[0, :, hq * hd:(hq + 1) * hd])
           for kh, hq in heads]
    p_l, alpha_l = [], []
    for (kh, hq), s in zip(heads, s_l):
        m_prev = m_scr[hq]
        m_new = jnp.maximum(m_prev, jnp.max(s, axis=0, keepdims=True))
        alpha = jnp.exp2(m_prev - m_new)
        p = jnp.exp2(s - m_new)
        l_scr[hq] = alpha * l_scr[hq] + jnp.sum(p, axis=0, keepdims=True)
        m_scr[hq] = m_new
        p_l.append(p.astype(BF16))
        alpha_l.append(alpha)
    for (kh, hq), p, alpha in zip(heads, p_l, alpha_l):
        pv = _dot_tn(v_ref[0, :, kh * hd:(kh + 1) * hd], p)
        acc_scr[hq] = alpha * acc_scr[hq] + pv

    @pl.when(j == pl.num_programs(3) - 1)
    def _():
        for hq in range(kvb * grp):
            o_ref[0, :, hq * hd:(hq + 1) * hd] = (acc_scr[hq] / l_scr[hq]).T.astype(BF16)


def _flash(q, k, v, hd, grp, kvb, tq, tk):
    b, nq, dq = q.shape
    nk, dk = k.shape[1], k.shape[2]
    nblk = dk // (kvb * hd)
    kern = functools.partial(_flash_kernel, kvb=kvb, grp=grp, hd=hd)
    nh = kvb * grp
    return pl.pallas_call(
        kern,
        grid=(b, nblk, nq // tq, nk // tk),
        in_specs=[pl.BlockSpec((1, tq, nh * hd), lambda b_, g, i, j: (b_, i, g)),
                  pl.BlockSpec((1, tk, kvb * hd), lambda b_, g, i, j: (b_, j, g)),
                  pl.BlockSpec((1, tk, kvb * hd), lambda b_, g, i, j: (b_, j, g))],
        out_specs=pl.BlockSpec((1, tq, nh * hd), lambda b_, g, i, j: (b_, i, g)),
        out_shape=jax.ShapeDtypeStruct((b, nq, dq), BF16),
        scratch_shapes=[pltpu.VMEM((nh, 1, tq), F32), pltpu.VMEM((nh, 1, tq), F32),
                        pltpu.VMEM((nh, hd, tq), F32)],
        compiler_params=_params("parallel", "parallel", "parallel", "arbitrary"),
        name="flash_attention",
    )(q, k, v)


def _na_qkv_kernel(x_ref, mod_ref, g_ref, w_ref, gq_ref, gk_ref, q_ref, k_ref, v_ref, *, nh, hd, scale):
    a = _norm_mod(x_ref[0], g_ref[...], mod_ref, 0).astype(BF16)
    qkv = _dot(a, w_ref[...])
    d = nh * hd
    ind, ind_t = _head_indicators(d, hd)

    def head_rms(x):
        return x * _dot3(lax.rsqrt(_dot3(x * x, ind) * (1.0 / hd) + RMS_EPS), ind_t)

    q_ref[0] = (head_rms(qkv[:, :d]) * (gq_ref[...] * scale)).astype(BF16)
    k_ref[0] = (head_rms(qkv[:, d:2 * d]) * gk_ref[...]).astype(BF16)
    v_ref[0] = qkv[:, 2 * d:].astype(BF16)


def _na_qkv(x, mod, g, w, gq, gk, tm):
    b, n, d = x.shape
    hd = NA_HEAD
    nh = w.shape[1] // (3 * hd)
    kern = functools.partial(_na_qkv_kernel, nh=nh, hd=hd, scale=hd ** -0.5 * LOG2_E)
    spec_o = pl.BlockSpec((1, tm, nh * hd), lambda b_, i: (b_, i, 0))
    return pl.pallas_call(
        kern,
        grid=(b, n // tm),
        in_specs=[pl.BlockSpec((1, tm, d), lambda b_, i: (b_, i, 0)),
                  pl.BlockSpec((1, 6, d), _mod_index(mod.shape[0])),
                  pl.BlockSpec((1, d), lambda b_, i: (0, 0)),
                  pl.BlockSpec(w.shape, lambda b_, i: (0, 0)),
                  pl.BlockSpec((1, nh * hd), lambda b_, i: (0, 0)),
                  pl.BlockSpec((1, nh * hd), lambda b_, i: (0, 0))],
        out_specs=[spec_o, spec_o, spec_o],
        out_shape=[jax.ShapeDtypeStruct((b, n, nh * hd), BF16)] * 3,
        compiler_params=_params("parallel", "parallel"),
        name="na_qkv",
    )(x, mod, g, w, jnp.tile(gq, (1, nh)), jnp.tile(gk, (1, nh)))


def _na_kernel(cls_ref, q_ref, k_ref, v_ref, kc_ref, vc_ref, bias_ref, o_ref, *, hb, hd, width, kh, rows):
    del cls_ref
    r = pl.program_id(2)
    rs = jnp.clip(r - kh // 2, 0, rows - kh)
    start = pl.multiple_of(rs * width, width)
    win = pl.ds(start, kh * width)
    hp = 128 // hd
    pairs = [slice(p * 128, (p + 1) * 128) for p in range(hb // hp)]
    lane = lax.broadcasted_iota(jnp.int32, (1, 128), 1)
    own = [(lane >= j * hd) & (lane < (j + 1) * hd) for j in range(hp)]
    heads = [(p, j) for p in range(len(pairs)) for j in range(hp)]
    q_l = [jnp.where(own[j], q_ref[0, :, pairs[p]], 0.0).astype(BF16) for p, j in heads]
    s_l = [_dot_nt(q_l[i], k_ref[0, win, pairs[p]]) for i, (p, j) in enumerate(heads)]
    sc_l = [_dot_nt(q_l[i], kc_ref[0, :, pairs[p]]) for i, (p, j) in enumerate(heads)]
    p_l, pc_l, l_l = [], [], []
    for i, (p, j) in enumerate(heads):
        s = s_l[i] + bias_ref[0, p * hp + j]
        sc = sc_l[i]
        m = jnp.maximum(jnp.max(s, axis=-1, keepdims=True), jnp.max(sc, axis=-1, keepdims=True))
        e = jnp.exp2(s - m)
        ec = jnp.exp2(sc - m)
        l_l.append(jnp.sum(e, axis=-1, keepdims=True) + jnp.sum(ec, axis=-1, keepdims=True))
        p_l.append(e.astype(BF16))
        pc_l.append(ec.astype(BF16))
    o_l = [_dot(p_l[i], v_ref[0, win, pairs[p]]) + _dot(pc_l[i], vc_ref[0, :, pairs[p]])
           for i, (p, j) in enumerate(heads)]
    for p in range(len(pairs)):
        o = o_l[p * hp] / l_l[p * hp]
        for j in range(1, hp):
            o = jnp.where(own[j], o_l[p * hp + j] / l_l[p * hp + j], o)
        o_ref[0, :, pairs[p]] = o.astype(BF16)


def _na_tables(rpb, rows):
    kh, kw = min(WIN_H, rows), WIN_W
    r_idx = np.arange(rows)
    delta = np.clip(r_idx - kh // 2, 0, rows - kh) - r_idx
    classes, cls_of_row = np.unique(delta, return_inverse=True)
    c_idx = np.arange(GRID_W)
    cs = np.clip(c_idx - kw // 2, 0, GRID_W - kw)
    kcol = np.arange(GRID_W)
    valid = (kcol[None, :] >= cs[:, None]) & (kcol[None, :] < cs[:, None] + kw)
    pad = GRID_W
    rp = jnp.pad(rpb.astype(F32), ((0, 0), (0, 0), (pad, pad)))
    toep = jnp.stack([rp[:, :, pad + WIN_W - 1 - c:pad + WIN_W - 1 - c + GRID_W] for c in range(GRID_W)], axis=2)
    toep = jnp.where(valid[None, None, :, :], toep * LOG2_E, NEG_BIG)
    tab = jnp.stack([toep[:, int(dl) + WIN_H - 1:int(dl) + WIN_H - 1 + kh] for dl in classes], axis=0)
    tab = tab.transpose(0, 1, 3, 2, 4).reshape(len(classes), rpb.shape[0], GRID_W, kh * GRID_W)
    return tab, jnp.asarray(cls_of_row, jnp.int32), kh


def _na_attention(q, k, v, kc, vc, rpb):
    b, t, d = q.shape
    c = kc.shape[1]
    hd = NA_HEAD
    hb = d // hd
    rows = t // GRID_W
    tab, cls_of_row, kh = _na_tables(rpb, rows)
    kern = functools.partial(_na_kernel, hb=hb, hd=hd, width=GRID_W, kh=kh, rows=rows)
    grid_spec = pltpu.PrefetchScalarGridSpec(
        num_scalar_prefetch=1,
        grid=(b, d // (hb * hd), rows),
        in_specs=[pl.BlockSpec((1, GRID_W, hb * hd), lambda b_, g, r, cls: (b_, r, g)),
                  pl.BlockSpec((1, t, hb * hd), lambda b_, g, r, cls: (b_, 0, g)),
                  pl.BlockSpec((1, t, hb * hd), lambda b_, g, r, cls: (b_, 0, g)),
                  pl.BlockSpec((1, c, hb * hd), lambda b_, g, r, cls: (b_, 0, g)),
                  pl.BlockSpec((1, c, hb * hd), lambda b_, g, r, cls: (b_, 0, g)),
                  pl.BlockSpec((1, hb, GRID_W, kh * GRID_W), lambda b_, g, r, cls: (cls[r], g, 0, 0))],
        out_specs=pl.BlockSpec((1, GRID_W, hb * hd), lambda b_, g, r, cls: (b_, r, g)),
    )
    return pl.pallas_call(
        kern,
        grid_spec=grid_spec,
        out_shape=jax.ShapeDtypeStruct((b, t, d), BF16),
        compiler_params=_params("parallel", "parallel", "arbitrary"),
        name="na_attention",
    )(cls_of_row, q, k, v, kc, vc, tab)


def _rw_feat_kernel(x_ref, xp_ref, xn_ref, mod_ref, g_ref, mu_ref, wr_ref, wk_ref, wv_ref,
                    w1_ref, w2_ref, a1_ref, a2_ref, g1_ref, g2_ref, w0_ref, a0_ref,
                    r_ref, k_ref, v_ref, lw_ref, a_ref, gate_ref, *, tm, lora_w, lora_a):
    i = pl.program_id(1)
    g = g_ref[...]
    a = _norm_mod(x_ref[0], g, mod_ref, 0)
    a_prev = _norm_mod(xp_ref[0], g, mod_ref, 0)[7:8, :]
    a_next = _norm_mod(xn_ref[0], g, mod_ref, 0)[0:1, :]
    a_prev = jnp.where(i == 0, 0.0, a_prev)
    a_next = jnp.where(i == pl.num_programs(1) - 1, 0.0, a_next)
    row = lax.broadcasted_iota(jnp.int32, (tm, 1), 0)
    prev = jnp.where(row == 0, a_prev, pltpu.roll(a, 1, 0))
    nxt = jnp.where(row == tm - 1, a_next, pltpu.roll(a, tm - 1, 0))
    xx = 0.5 * (prev + nxt) - a

    def mix(j):
        return (a + xx * mu_ref[j:j + 1, :]).astype(BF16)

    r_ref[0] = _dot(mix(0), wr_ref[...])
    k_ref[0] = _dot(mix(2), wk_ref[...])
    v_ref[0] = _dot(mix(3), wv_ref[...])
    tw = jnp.tanh(_dot(mix(1), w1_ref[...]))
    ta = _dot(mix(4), a1_ref[...])
    for e in range(2):
        lora = _dot(tw[:, e * lora_w:(e + 1) * lora_w].astype(BF16), w2_ref[e])
        lw_ref[e, 0] = -EXP_M05 * _sigmoid(w0_ref[e:e + 1, :] + lora)
        la = _dot(ta[:, e * lora_a:(e + 1) * lora_a].astype(BF16), a2_ref[e])
        a_ref[e, 0] = _sigmoid(a0_ref[e:e + 1, :] + la)
    gate_ref[0] = _dot(_sigmoid(_dot(mix(5), g1_ref[...])).astype(BF16), g2_ref[...])


def _rw_features(x, mod, g, p, tm):
    b, n, d = x.shape
    nb8 = n // 8
    tb = tm // 8
    lora_w = p["w2"].shape[1]
    lora_a = p["a2"].shape[1]
    kern = functools.partial(_rw_feat_kernel, tm=tm, lora_w=lora_w, lora_a=lora_a)
    full = lambda arr: pl.BlockSpec(arr.shape, lambda b_, i: (0,) * arr.ndim)
    tile = pl.BlockSpec((1, tm, d), lambda b_, i: (b_, i, 0))
    tile2 = pl.BlockSpec((2, 1, tm, d), lambda b_, i: (0, b_, i, 0))
    names = ("mu", "wr", "wk", "wv", "w1", "w2", "a1", "a2", "g1", "g2", "w0", "a0")
    return pl.pallas_call(
        kern,
        grid=(b, n // tm),
        in_specs=[tile,
                  pl.BlockSpec((1, 8, d), lambda b_, i: (b_, jnp.maximum(i * tb - 1, 0), 0)),
                  pl.BlockSpec((1, 8, d), lambda b_, i: (b_, jnp.minimum((i + 1) * tb, nb8 - 1), 0)),
                  pl.BlockSpec((1, 6, d), _mod_index(mod.shape[0])),
                  pl.BlockSpec((1, d), lambda b_, i: (0, 0))] + [full(p[nm]) for nm in names],
        out_specs=[tile, tile, tile, tile2, tile2, tile],
        out_shape=[jax.ShapeDtypeStruct((b, n, d), F32)] * 3
        + [jax.ShapeDtypeStruct((2, b, n, d), F32)] * 2 + [jax.ShapeDtypeStruct((b, n, d), F32)],
        compiler_params=_params("parallel", "parallel"),
        name="rwkv_features",
    )(x, x, x, mod, g, *[p[nm] for nm in names])


def _rw_scan_kernel(s0_ref, rf_ref, kf_ref, vf_ref, rr_ref, kr_ref, vr_ref, lwf_ref, lwr_ref, af_ref, ar_ref,
                    kkp_ref, kap_ref, rkp_ref, yf_ref, yr_ref, bonf_ref, bonr_ref, sf_ref, s_scr, *, L, H, N):
    c = pl.program_id(1)

    @pl.when(c == 0)
    def _():
        s_scr[...] = s0_ref[:, 0]

    D = H * N
    ri = lax.broadcasted_iota(jnp.int32, (L, L), 0)
    ci = lax.broadcasted_iota(jnp.int32, (L, L), 1)
    eye = jnp.where(ri == ci, 1.0, 0.0).astype(F32)
    ri2 = lax.broadcasted_iota(jnp.int32, (2 * L, 2 * L), 0)
    ci2 = lax.broadcasted_iota(jnp.int32, (2 * L, 2 * L), 1)
    rr = ri2 & (L - 1)
    cc = ci2 & (L - 1)
    strict_rows = jnp.where(ri2 < L, 1, 0)
    ind = jnp.where(lax.broadcasted_iota(jnp.int32, (D, 128), 0) // N
                    == lax.broadcasted_iota(jnp.int32, (D, 128), 1), 1.0, 0.0).astype(BF16)
    n_fact = int(math.log2(L))
    sls = [slice(h * N, (h + 1) * N) for h in range(H)]

    dirs = []
    for rev, (r_ref, k_ref, v_ref, lw_ref, a_ref) in enumerate(
            ((rf_ref, kf_ref, vf_ref, lwf_ref, af_ref), (rr_ref, kr_ref, vr_ref, lwr_ref, ar_ref))):
        dist = (ci - ri) if rev else (ri - ci)
        incl_b = jnp.where(dist >= 0, 1.0, 0.0).astype(BF16)
        mask2 = ((cc - rr) if rev else (rr - cc)) >= strict_rows
        lw = lw_ref[0, 0]
        l1, l2, l3 = _split3(lw)
        cw = _dot(incl_b, l1) + _dot(incl_b, l2) + _dot(incl_b, l3)
        tot = cw[0:1, :] if rev else cw[L - 1:L, :]
        a = a_ref[0, 0]
        k = k_ref[0]
        r = r_ref[0]
        v = v_ref[0]
        kk_raw = k * kkp_ref[...]
        kd = k * (1.0 + (a - 1.0) * kap_ref[...])
        t1, t2, t3 = _split3(jnp.concatenate([kk_raw * kk_raw, r * kd * rkp_ref[...]], axis=0))
        hs = _dot(t1, ind) + _dot(t2, ind) + _dot(t3, ind)
        dirs.append(dict(
            mask2=mask2, a=a, v=v, kk_raw=kk_raw,
            inv_norm=jnp.minimum(lax.rsqrt(hs[:L]), 1.0 / NORM_EPS), rk_sum=hs[L:],
            e_prev=jnp.exp(cw - lw), e_neg=jnp.exp(-cw), e_rem=jnp.exp(tot - cw), e_tot=jnp.exp(tot),
            r_t=r * jnp.exp(cw), k_t=kd * jnp.exp(-cw), k_h=kd * jnp.exp(tot - cw)))

    items = [(e, h) for e in range(2) for h in range(H)]
    x_ar, z_bk, b_l, vb = [], [], [], []
    for e, h in items:
        d, sl = dirs[e], sls[h]
        kk = d["kk_raw"][:, sl] * d["inv_norm"][:, h:h + 1]
        b = kk * d["a"][:, sl]
        b_l.append(b)
        x_ar.append(jnp.concatenate([-kk * d["e_prev"][:, sl], d["r_t"][:, sl]], axis=0).astype(BF16))
        z_bk.append(jnp.concatenate([b * d["e_neg"][:, sl], d["k_t"][:, sl]], axis=0).astype(BF16))
        vb.append(d["v"][:, sl].astype(BF16))
    idx = range(len(items))
    gram = [jnp.where(dirs[items[i][0]]["mask2"], _dot_nt(x_ar[i], z_bk[i]), 0.0) for i in idx]
    s_old = [s_scr[e, h] for e, h in items]
    xs = [_dot_nt(x_ar[i], s_old[i].astype(BF16)) for i in idx]
    rhs = [xs[i][:L] + _dot(gram[i][:L, L:].astype(BF16), vb[i]) for i in idx]

    qb = [gram[i][:L, :L].astype(BF16) for i in idx]
    t_inv = [eye + gram[i][:L, :L] for i in idx]
    q = [_dot(qb[i], qb[i]) for i in idx]
    for _ in range(n_fact - 2):
        qb = [q[i].astype(BF16) for i in idx]
        st = [_dot(jnp.concatenate([t_inv[i].astype(BF16), qb[i]], axis=0), qb[i]) for i in idx]
        t_inv = [t_inv[i] + st[i][:L] for i in idx]
        q = [st[i][L:] for i in idx]
    st = [_dot(t_inv[i].astype(BF16), q[i].astype(BF16)) for i in idx]
    t_inv = [t_inv[i] + st[i] for i in idx]

    u = [_dot(t_inv[i].astype(BF16), rhs[i].astype(BF16)) for i in idx]
    uv = [jnp.concatenate([u[i].astype(BF16), vb[i]], axis=0) for i in idx]
    y = [xs[i][L:] + _dot(gram[i][L:, :].astype(BF16), uv[i]) for i in idx]
    bk = [jnp.concatenate([b_l[i] * dirs[e]["e_rem"][:, sls[h]], dirs[e]["k_h"][:, sls[h]]], axis=0).astype(BF16)
          for i, (e, h) in enumerate(items)]
    s_new = [s_old[i] * dirs[e]["e_tot"][:, sls[h]] + _dot_tn(uv[i], bk[i]) for i, (e, h) in enumerate(items)]
    for i, (e, h) in enumerate(items):
        sl = sls[h]
        y_ref, bon_ref = (yr_ref, bonr_ref) if e else (yf_ref, bonf_ref)
        y_ref[0, :, sl] = y[i]
        s_scr[e, h] = s_new[i]
        bon_ref[0, :, sl] = dirs[e]["rk_sum"][:, h:h + 1] * dirs[e]["v"][:, sl]

    @pl.when(c == pl.num_programs(1) - 1)
    def _():
        sf_ref[:, 0] = s_scr[...]


def _rw_scan(s0, r, k, v, lw, a, kkp, kap, rkp):
    b, n, d = r.shape
    L = SCAN_CHUNK
    N = RW_HEAD
    H = d // N
    nc = n // L
    kern = functools.partial(_rw_scan_kernel, L=L, H=H, N=N)
    tok_f = pl.BlockSpec((1, L, d), lambda b_, c: (b_, c, 0))
    tok_r = pl.BlockSpec((1, L, d), lambda b_, c: (b_, nc - 1 - c, 0))
    dir_f = pl.BlockSpec((1, 1, L, d), lambda b_, c: (0, b_, c, 0))
    dir_r = pl.BlockSpec((1, 1, L, d), lambda b_, c: (1, b_, nc - 1 - c, 0))
    vec = pl.BlockSpec((1, d), lambda b_, c: (0, 0))
    st = pl.BlockSpec((2, 1, H, N, N), lambda b_, c: (0, b_, 0, 0, 0))
    tok_shape = jax.ShapeDtypeStruct((b, n, d), F32)
    return pl.pallas_call(
        kern,
        grid=(b, nc),
        in_specs=[st, tok_f, tok_f, tok_f, tok_r, tok_r, tok_r, dir_f, dir_r, dir_f, dir_r, vec, vec, vec],
        out_specs=[tok_f, tok_r, tok_f, tok_r, st],
        out_shape=[tok_shape, tok_shape, tok_shape, tok_shape, jax.ShapeDtypeStruct((2, b, H, N, N), F32)],
        scratch_shapes=[pltpu.VMEM((2, H, N, N), F32)],
        compiler_params=_params("parallel", "arbitrary"),
        name="rwkv_scan",
    )(s0, r, k, v, r, k, v, lw, lw, a, a, kkp, kap, rkp)


def _rw_out_kernel(yf_ref, yr_ref, bonf_ref, bonr_ref, gate_ref, res_ref, mod_ref, lg_ref, lb_ref, wo_ref, o_ref,
                   *, H, N):
    y = yf_ref[0] + yr_ref[0]
    ind, ind_t = _head_indicators(H * N, N)
    dlt = y - _dot3(_dot3(y, ind) * (1.0 / N), ind_t)
    var = _dot3(dlt * dlt, ind) * (1.0 / N)
    yn = dlt * _dot3(lax.rsqrt(var + LNX_EPS), ind_t)
    z = (yn * lg_ref[...] + lb_ref[...] + (bonf_ref[0] + bonr_ref[0])) * gate_ref[0]
    o_ref[0] = res_ref[0] + mod_ref[0, 2:3, :] * _dot(z.astype(BF16), wo_ref[...])


def _rw_readout(ys, bons, gate, res, mod, lg, lb, wo, tm):
    b, n, d = res.shape
    N = RW_HEAD
    kern = functools.partial(_rw_out_kernel, H=d // N, N=N)
    tile = pl.BlockSpec((1, tm, d), lambda b_, i: (b_, i, 0))
    vec = pl.BlockSpec((1, d), lambda b_, i: (0, 0))
    return pl.pallas_call(
        kern,
        grid=(b, n // tm),
        in_specs=[tile, tile, tile, tile, tile, tile, pl.BlockSpec((1, 6, d), _mod_index(mod.shape[0])),
                  vec, vec, pl.BlockSpec((d, d), lambda b_, i: (0, 0))],
        out_specs=tile,
        out_shape=jax.ShapeDtypeStruct((b, n, d), F32),
        compiler_params=_params("parallel", "parallel"),
        name="rwkv_readout",
    )(*ys, *bons, gate, res, mod, lg, lb, wo)


def _rope_tables(n_tok, hd):
    t = jnp.arange(n_tok)
    rows = (t // GRID_W).astype(F32)
    cols = (t % GRID_W).astype(F32)
    d_axis = hd // 2
    inv = jnp.float32(ROPE_BASE) ** (-jnp.arange(0, d_axis, 2, dtype=F32) / d_axis)
    ang = jnp.concatenate([rows[:, None] * inv, cols[:, None] * inv], axis=-1)
    cos = jnp.repeat(jnp.cos(ang), 2, axis=-1)
    sign = jnp.tile(jnp.asarray([-1.0, 1.0], F32), hd // 2)
    sin = jnp.repeat(jnp.sin(ang), 2, axis=-1) * sign
    return cos, sin


def _tiles(n):
    return min(n, 512), min(n, 1024)


def kernel(x, c, ctx, c_ctx, mod_w, mod_b, norm_mix, norm_ffn, ff_w1, ff_w3, ff_w2, rw_mu, rw_wr, rw_wk, rw_wv, rw_wo, rw_w0, rw_w1, rw_w2, rw_a0, rw_a1, rw_a2, rw_g1, rw_g2, rw_kk, rw_ka, rw_rk, rw_lnx_g, rw_lnx_b, at_wq, at_wk, at_wv, at_wo, at_gq, at_gk, na_wqkv, na_wo, na_gq, na_gk, na_rpb):
    B, T, D = x.shape
    C = ctx.shape[1]
    depth = mod_w.shape[0]
    bf = lambda w: w.astype(BF16)

    cc = jnp.concatenate([c, c_ctx[None, :], jnp.zeros((8 - B - 1, D), F32)], axis=0)
    mods = _modulation(cc, mod_w, mod_b)
    mod_lat = mods[:, :B].reshape(depth, B, 6, D)
    mod_ctx = mods[:, B:B + 1].reshape(depth, 1, 6, D)

    tl, tl_ffn = _tiles(T)
    tc, _ = _tiles(C)
    h_lat, h_ctx = x, ctx
    for i in range(depth):
        need_ctx = i < depth - 1
        kind, j = i % 3, i // 3
        ml, mc = mod_lat[i], mod_ctx[i]
        g_mix = norm_mix[i][None, :]
        if kind == 0:
            cat = lambda w: jnp.concatenate([w[0], w[1]], axis=1)
            p = dict(mu=rw_mu[j], wr=bf(rw_wr[j]), wk=bf(rw_wk[j]), wv=bf(rw_wv[j]),
                     w1=bf(cat(rw_w1[j])), w2=bf(rw_w2[j]), a1=bf(cat(rw_a1[j])), a2=bf(rw_a2[j]),
                     g1=bf(rw_g1[j]), g2=bf(rw_g2[j]), w0=rw_w0[j], a0=rw_a0[j])
            kkp, kap, rkp = rw_kk[j][None, :], rw_ka[j][None, :], rw_rk[j].reshape(1, D)
            lg, lb, wo = rw_lnx_g[j][None, :], rw_lnx_b[j][None, :], bf(rw_wo[j])
            r_c, k_c, v_c, lw_c, a_c, gate_c = _rw_features(h_ctx, mc, g_mix, p, min(C, 256))
            r_l, k_l, v_l, lw_l, a_l, gate_l = _rw_features(h_lat, ml, g_mix, p, min(T, 256))
            s0 = jnp.zeros((2, B, D // RW_HEAD, RW_HEAD, RW_HEAD), F32)
            *out_c, s_c = _rw_scan(s0, r_c, k_c, v_c, lw_c, a_c, kkp, kap, rkp)
            *out_l, _ = _rw_scan(s_c, r_l, k_l, v_l, lw_l, a_l, kkp, kap, rkp)
            h_lat = _rw_readout(out_l[:2], out_l[2:], gate_l, h_lat, ml, lg, lb, wo, min(T, 256))
            if need_ctx:
                h_ctx = _rw_readout(out_c[:2], out_c[2:], gate_c, h_ctx, mc, lg, lb, wo, min(C, 256))
        elif kind == 1:
            w = bf(jnp.concatenate([at_wq[j], at_wk[j], at_wv[j]], axis=1))
            gq, gk, wo = at_gq[j][None, :], at_gk[j][None, :], bf(at_wo[j])
            cos, sin = _rope_tables(T, AT_HEAD)
            q_l, k_l, v_l = _gqa_qkv(h_lat, ml, g_mix, w, gq, gk, cos, sin, tl, True)
            q_c, k_c, v_c = _gqa_qkv(h_ctx, mc, g_mix, w, gq, gk, cos, sin, tc, False)
            k_all = jnp.concatenate([k_l, k_c], axis=1)
            v_all = jnp.concatenate([v_l, v_c], axis=1)
            grp = q_l.shape[2] // k_l.shape[2]
            o_l = _flash(q_l, k_all, v_all, AT_HEAD, grp, 1, tl, math.gcd(T + C, 256))
            h_lat = _oproj(o_l, h_lat, ml, wo, tl)
            if need_ctx:
                o_c = _flash(q_c, k_c, v_c, AT_HEAD, grp, 1, tc, tc)
                h_ctx = _oproj(o_c, h_ctx, mc, wo, tc)
        else:
            w, wo = bf(na_wqkv[j]), bf(na_wo[j])
            gq, gk = na_gq[j][None, :], na_gk[j][None, :]
            q_l, k_l, v_l = _na_qkv(h_lat, ml, g_mix, w, gq, gk, min(T, 256))
            q_c, k_c, v_c = _na_qkv(h_ctx, mc, g_mix, w, gq, gk, min(C, 256))
            o_l = _na_attention(q_l, k_l, v_l, k_c, v_c, na_rpb[j])
            h_lat = _oproj(o_l, h_lat, ml, wo, tl)
            if need_ctx:
                o_c = _flash(q_c, k_c, v_c, NA_HEAD, 1, 2, tc, tc)
                h_ctx = _oproj(o_c, h_ctx, mc, wo, tc)
        g_ffn = norm_ffn[i][None, :]
        w1, w3, w2 = bf(ff_w1[i]), bf(ff_w3[i]), bf(ff_w2[i])
        h_lat = _ffn(h_lat, ml, g_ffn, w1, w3, w2, tl_ffn)
        if need_ctx:
            h_ctx = _ffn(h_ctx.reshape(1, B * C, D), mc, g_ffn, w1, w3, w2, min(B * C, 1024)).reshape(B, C, D)
    return h_lat
```

```python
import functools
import math

import numpy as np
import jax
import jax.numpy as jnp
from jax import lax
from jax.experimental import pallas as pl
from jax.experimental.pallas import tpu as pltpu

F32 = jnp.float32
BF16 = jnp.bfloat16

GRID_W = 64
RW_HEAD = 64
AT_HEAD = 128
AT_KV_HEADS = 2
NA_HEAD = 64
WIN_H = 8
WIN_W = 16
ROPE_BASE = 10000.0
RMS_EPS = 1e-6
LNX_EPS = 64e-5
NORM_EPS = 1e-12
NEG_BIG = -1e30
EXP_M05 = math.exp(-0.5)
LOG2_E = math.log2(math.e)

SCAN_CHUNK = 64
PAIR_LANES = 128
VMEM_LIMIT_BYTES_V7X = 48 * 1024 * 1024


def _params(*sem):
    return pltpu.CompilerParams(dimension_semantics=sem, vmem_limit_bytes=VMEM_LIMIT_BYTES_V7X)


def _sigmoid(x):
    return 1.0 / (1.0 + jnp.exp(-x))


def _rms(x, eps):
    return x * lax.rsqrt(jnp.mean(x * x, axis=-1, keepdims=True) + eps)


def _norm_mod(x, g, mod_ref, k):
    return (_rms(x, RMS_EPS) * g) * (1.0 + mod_ref[0, k + 1:k + 2, :]) + mod_ref[0, k:k + 1, :]


def _dot(a, b):
    return jnp.dot(a, b, preferred_element_type=F32)


def _dot_nt(a, b):
    return lax.dot_general(a, b, (((1,), (1,)), ((), ())), preferred_element_type=F32)


def _dot_tn(a, b):
    return lax.dot_general(a, b, (((0,), (0,)), ((), ())), preferred_element_type=F32)


def _split3(x):
    h1 = x.astype(BF16)
    r1 = x - h1.astype(F32)
    h2 = r1.astype(BF16)
    h3 = (r1 - h2.astype(F32)).astype(BF16)
    return h1, h2, h3


def _dot3(x, m01):
    x1, x2, x3 = _split3(x)
    return _dot(x1, m01) + _dot(x2, m01) + _dot(x3, m01)


def _head_indicators(d, n):
    ind = jnp.where(lax.broadcasted_iota(jnp.int32, (d, 128), 0) // n
                    == lax.broadcasted_iota(jnp.int32, (d, 128), 1), 1.0, 0.0).astype(BF16)
    ind_t = jnp.where(lax.broadcasted_iota(jnp.int32, (128, d), 1) // n
                      == lax.broadcasted_iota(jnp.int32, (128, d), 0), 1.0, 0.0).astype(BF16)
    return ind, ind_t


def _mod_index(bm):
    return (lambda b, *_: (b, 0, 0)) if bm > 1 else (lambda b, *_: (0, 0, 0))


def _mod_kernel(x_ref, w_ref, b_ref, o_ref):
    x = x_ref[...]
    s = x * _sigmoid(x)
    w = w_ref[0]
    sh, sm, _ = _split3(s)
    wh, wm, _ = _split3(w)
    o_ref[0] = _dot(sh, wh) + _dot(sh, wm) + _dot(sm, wh) + b_ref[0]


def _modulation(cc, mod_w, mod_b):
    depth, d, e = mod_w.shape
    tn = 1536
    return pl.pallas_call(
        _mod_kernel,
        grid=(depth, e // tn),
        in_specs=[pl.BlockSpec((8, d), lambda l, j: (0, 0)),
                  pl.BlockSpec((1, d, tn), lambda l, j: (l, 0, j)),
                  pl.BlockSpec((1, 1, tn), lambda l, j: (l, 0, j))],
        out_specs=pl.BlockSpec((1, 8, tn), lambda l, j: (l, 0, j)),
        out_shape=jax.ShapeDtypeStruct((depth, 8, e), F32),
        compiler_params=_params("parallel", "parallel"),
        name="modulation",
    )(cc, mod_w, mod_b.reshape(depth, 1, e))


def _ffn_kernel(x_ref, mod_ref, g_ref, w1_ref, w3_ref, w2_ref, o_ref, f_scr, acc_scr):
    j = pl.program_id(2)

    @pl.when(j == 0)
    def _():
        f_scr[...] = _norm_mod(x_ref[0], g_ref[...], mod_ref, 3).astype(BF16)
        acc_scr[...] = jnp.zeros_like(acc_scr)

    f = f_scr[...]
    h1 = _dot(f, w1_ref[...])
    h3 = _dot(f, w3_ref[...])
    hm = (h1 * _sigmoid(h1)) * h3
    acc_scr[...] += _dot(hm.astype(BF16), w2_ref[...])

    @pl.when(j == pl.num_programs(2) - 1)
    def _():
        o_ref[0] = x_ref[0] + mod_ref[0, 5:6, :] * acc_scr[...]


def _ffn(x, mod, g, w1, w3, w2, tm):
    b, n, d = x.shape
    f = w1.shape[1]
    tf = 256
    return pl.pallas_call(
        _ffn_kernel,
        grid=(b, n // tm, f // tf),
        in_specs=[pl.BlockSpec((1, tm, d), lambda b_, i, j: (b_, i, 0)),
                  pl.BlockSpec((1, 6, d), _mod_index(mod.shape[0])),
                  pl.BlockSpec((1, d), lambda b_, i, j: (0, 0)),
                  pl.BlockSpec((d, tf), lambda b_, i, j: (0, j)),
                  pl.BlockSpec((d, tf), lambda b_, i, j: (0, j)),
                  pl.BlockSpec((tf, d), lambda b_, i, j: (j, 0))],
        out_specs=pl.BlockSpec((1, tm, d), lambda b_, i, j: (b_, i, 0)),
        out_shape=jax.ShapeDtypeStruct((b, n, d), F32),
        scratch_shapes=[pltpu.VMEM((tm, d), BF16), pltpu.VMEM((tm, d), F32)],
        compiler_params=_params("parallel", "parallel", "arbitrary"),
        name="ffn",
    )(x, mod, g, w1, w3, w2)


def _oproj_kernel(x_ref, res_ref, mod_ref, w_ref, o_ref):
    o_ref[0] = res_ref[0] + mod_ref[0, 2:3, :] * _dot(x_ref[0], w_ref[...])


def _oproj(x, res, mod, w, tm):
    b, n, k = x.shape
    d = w.shape[1]
    return pl.pallas_call(
        _oproj_kernel,
        grid=(b, n // tm),
        in_specs=[pl.BlockSpec((1, tm, k), lambda b_, i: (b_, i, 0)),
                  pl.BlockSpec((1, tm, d), lambda b_, i: (b_, i, 0)),
                  pl.BlockSpec((1, 6, d), _mod_index(mod.shape[0])),
                  pl.BlockSpec((k, d), lambda b_, i: (0, 0))],
        out_specs=pl.BlockSpec((1, tm, d), lambda b_, i: (b_, i, 0)),
        out_shape=jax.ShapeDtypeStruct((b, n, d), F32),
        compiler_params=_params("parallel", "parallel"),
        name="oproj",
    )(x, res, mod, w)


def _gqa_qkv_kernel(x_ref, mod_ref, g_ref, w_ref, gq_ref, gk_ref, cos_ref, sin_ref,
                    q_ref, k_ref, v_ref, *, rope, n_q, n_kv, hd, scale):
    a = _norm_mod(x_ref[0], g_ref[...], mod_ref, 0).astype(BF16)
    qkv = _dot(a, w_ref[...])
    if rope:
        cos = cos_ref[...]
        sin = sin_ref[...]
        even = (lax.broadcasted_iota(jnp.int32, cos.shape, 1) % 2) == 0

    def head(xh, g):
        y = _rms(xh, RMS_EPS) * g
        if rope:
            partner = jnp.where(even, pltpu.roll(y, hd - 1, 1), pltpu.roll(y, 1, 1))
            y = y * cos + partner * sin
        return y

    gq = gq_ref[...]
    gk = gk_ref[...]
    for h in range(n_q):
        sl = slice(h * hd, (h + 1) * hd)
        q_ref[0, :, sl] = (head(qkv[:, sl], gq) * scale).astype(BF16)
    for h in range(n_kv):
        sl = slice(h * hd, (h + 1) * hd)
        k_ref[0, :, sl] = head(qkv[:, n_q * hd + h * hd:n_q * hd + (h + 1) * hd], gk).astype(BF16)
    v_ref[0] = qkv[:, (n_q + n_kv) * hd:].astype(BF16)


def _gqa_qkv(x, mod, g, w, gq, gk, cos, sin, tm, rope):
    b, n, d = x.shape
    hd = AT_HEAD
    n_kv = AT_KV_HEADS
    n_q = w.shape[1] // hd - 2 * n_kv
    kern = functools.partial(_gqa_qkv_kernel, rope=rope, n_q=n_q, n_kv=n_kv, hd=hd, scale=hd ** -0.5 * LOG2_E)
    tab = (lambda b_, i: (i, 0)) if rope else (lambda b_, i: (0, 0))
    return pl.pallas_call(
        kern,
        grid=(b, n // tm),
        in_specs=[pl.BlockSpec((1, tm, d), lambda b_, i: (b_, i, 0)),
                  pl.BlockSpec((1, 6, d), _mod_index(mod.shape[0])),
                  pl.BlockSpec((1, d), lambda b_, i: (0, 0)),
                  pl.BlockSpec(w.shape, lambda b_, i: (0, 0)),
                  pl.BlockSpec((1, hd), lambda b_, i: (0, 0)),
                  pl.BlockSpec((1, hd), lambda b_, i: (0, 0)),
                  pl.BlockSpec((tm, hd), tab),
                  pl.BlockSpec((tm, hd), tab)],
        out_specs=[pl.BlockSpec((1, tm, n_q * hd), lambda b_, i: (b_, i, 0)),
                   pl.BlockSpec((1, tm, n_kv * hd), lambda b_, i: (b_, i, 0)),
                   pl.BlockSpec((1, tm, n_kv * hd), lambda b_, i: (b_, i, 0))],
        out_shape=[jax.ShapeDtypeStruct((b, n, n_q * hd), BF16),
                   jax.ShapeDtypeStruct((b, n, n_kv * hd), BF16),
                   jax.ShapeDtypeStruct((b, n, n_kv * hd), BF16)],
        compiler_params=_params("parallel", "parallel"),
        name="gqa_qkv",
    )(x, mod, g, w, gq, gk, cos, sin)


def _flash_kernel(q_ref, k_ref, v_ref, o_ref, m_scr, l_scr, acc_scr, *, kvb, grp, hd):
    j = pl.program_id(3)

    @pl.when(j == 0)
    def _():
        m_scr[...] = jnp.full_like(m_scr, NEG_BIG)
        l_scr[...] = jnp.zeros_like(l_scr)
        acc_scr[...] = jnp.zeros_like(acc_scr)

    heads = [(kh, kh * grp + g) for kh in range(kvb) for g in range(grp)]
    s_l = [_dot_nt(k_ref[0, :, kh * hd:(kh + 1) * hd], q_ref[0, :, hq * hd:(hq + 1) * hd])
           for kh, hq in heads]
    p_l, alpha_l = [], []
    for (kh, hq), s in zip(heads, s_l):
        m_prev = m_scr[hq]
        m_new = jnp.maximum(m_prev, jnp.max(s, axis=0, keepdims=True))
        alpha = jnp.exp2(m_prev - m_new)
        p = jnp.exp2(s - m_new)
        l_scr[hq] = alpha * l_scr[hq] + jnp.sum(p, axis=0, keepdims=True)
        m_scr[hq] = m_new
        p_l.append(p.astype(BF16))
        alpha_l.append(alpha)
    for (kh, hq), p, alpha in zip(heads, p_l, alpha_l):
        pv = _dot_tn(v_ref[0, :, kh * hd:(kh + 1) * hd], p)
        acc_scr[hq] = alpha * acc_scr[hq] + pv

    @pl.when(j == pl.num_programs(3) - 1)
    def _():
        for hq in range(kvb * grp):
            o_ref[0, :, hq * hd:(hq + 1) * hd] = (acc_scr[hq] / l_scr[hq]).T.astype(BF16)


def _flash(q, k, v, hd, grp, kvb, tq, tk):
    b, nq, dq = q.shape
    nk, dk = k.shape[1], k.shape[2]
    nblk = dk // (kvb * hd)
    kern = functools.partial(_flash_kernel, kvb=kvb, grp=grp, hd=hd)
    nh = kvb * grp
    return pl.pallas_call(
        kern,
        grid=(b, nblk, nq // tq, nk // tk),
        in_specs=[pl.BlockSpec((1, tq, nh * hd), lambda b_, g, i, j: (b_, i, g)),
                  pl.BlockSpec((1, tk, kvb * hd), lambda b_, g, i, j: (b_, j, g)),
                  pl.BlockSpec((1, tk, kvb * hd), lambda b_, g, i, j: (b_, j, g))],
        out_specs=pl.BlockSpec((1, tq, nh * hd), lambda b_, g, i, j: (b_, i, g)),
        out_shape=jax.ShapeDtypeStruct((b, nq, dq), BF16),
        scratch_shapes=[pltpu.VMEM((nh, 1, tq), F32), pltpu.VMEM((nh, 1, tq), F32),
                        pltpu.VMEM((nh, hd, tq), F32)],
        compiler_params=_params("parallel", "parallel", "parallel", "arbitrary"),
        name="flash_attention",
    )(q, k, v)


def _na_qkv_kernel(x_ref, mod_ref, g_ref, w_ref, gq_ref, gk_ref, q_ref, k_ref, v_ref, *, nh, hd, scale):
    a = _norm_mod(x_ref[0], g_ref[...], mod_ref, 0).astype(BF16)
    qkv = _dot(a, w_ref[...])
    d = nh * hd
    ind, ind_t = _head_indicators(d, hd)

    def head_rms(x):
        return x * _dot3(lax.rsqrt(_dot3(x * x, ind) * (1.0 / hd) + RMS_EPS), ind_t)

    q_ref[0] = (head_rms(qkv[:, :d]) * (gq_ref[...] * scale)).astype(BF16)
    k_ref[0] = (head_rms(qkv[:, d:2 * d]) * gk_ref[...]).astype(BF16)
    v_ref[0] = qkv[:, 2 * d:].astype(BF16)


def _na_qkv(x, mod, g, w, gq, gk, tm):
    b, n, d = x.shape
    hd = NA_HEAD
    nh = w.shape[1] // (3 * hd)
    kern = functools.partial(_na_qkv_kernel, nh=nh, hd=hd, scale=hd ** -0.5 * LOG2_E)
    spec_o = pl.BlockSpec((1, tm, nh * hd), lambda b_, i: (b_, i, 0))
    return pl.pallas_call(
        kern,
        grid=(b, n // tm),
        in_specs=[pl.BlockSpec((1, tm, d), lambda b_, i: (b_, i, 0)),
                  pl.BlockSpec((1, 6, d), _mod_index(mod.shape[0])),
                  pl.BlockSpec((1, d), lambda b_, i: (0, 0)),
                  pl.BlockSpec(w.shape, lambda b_, i: (0, 0)),
                  pl.BlockSpec((1, nh * hd), lambda b_, i: (0, 0)),
                  pl.BlockSpec((1, nh * hd), lambda b_, i: (0, 0))],
        out_specs=[spec_o, spec_o, spec_o],
        out_shape=[jax.ShapeDtypeStruct((b, n, nh * hd), BF16)] * 3,
        compiler_params=_params("parallel", "parallel"),
        name="na_qkv",
    )(x, mod, g, w, jnp.tile(gq, (1, nh)), jnp.tile(gk, (1, nh)))


def _na_kernel(cls_ref, q_ref, k_ref, v_ref, kc_ref, vc_ref, bias_ref, o_ref, *, hb, hd, width, kh, rows):
    del cls_ref
    r = pl.program_id(2)
    rs = jnp.clip(r - kh // 2, 0, rows - kh)
    start = pl.multiple_of(rs * width, width)
    win = pl.ds(start, kh * width)
    hp = 128 // hd
    pairs = [slice(p * 128, (p + 1) * 128) for p in range(hb // hp)]
    lane = lax.broadcasted_iota(jnp.int32, (1, 128), 1)
    own = [(lane >= j * hd) & (lane < (j + 1) * hd) for j in range(hp)]
    heads = [(p, j) for p in range(len(pairs)) for j in range(hp)]
    q_l = [jnp.where(own[j], q_ref[0, :, pairs[p]], 0.0).astype(BF16) for p, j in heads]
    s_l = [_dot_nt(q_l[i], k_ref[0, win, pairs[p]]) for i, (p, j) in enumerate(heads)]
    sc_l = [_dot_nt(q_l[i], kc_ref[0, :, pairs[p]]) for i, (p, j) in enumerate(heads)]
    p_l, pc_l, l_l = [], [], []
    for i, (p, j) in enumerate(heads):
        s = s_l[i] + bias_ref[0, p * hp + j]
        sc = sc_l[i]
        m = jnp.maximum(jnp.max(s, axis=-1, keepdims=True), jnp.max(sc, axis=-1, keepdims=True))
        e = jnp.exp2(s - m)
        ec = jnp.exp2(sc - m)
        l_l.append(jnp.sum(e, axis=-1, keepdims=True) + jnp.sum(ec, axis=-1, keepdims=True))
        p_l.append(e.astype(BF16))
        pc_l.append(ec.astype(BF16))
    o_l = [_dot(p_l[i], v_ref[0, win, pairs[p]]) + _dot(pc_l[i], vc_ref[0, :, pairs[p]])
           for i, (p, j) in enumerate(heads)]
    for p in range(len(pairs)):
        o = o_l[p * hp] / l_l[p * hp]
        for j in range(1, hp):
            o = jnp.where(own[j], o_l[p * hp + j] / l_l[p * hp + j], o)
        o_ref[0, :, pairs[p]] = o.astype(BF16)


def _na_tables(rpb, rows):
    kh, kw = min(WIN_H, rows), WIN_W
    r_idx = np.arange(rows)
    delta = np.clip(r_idx - kh // 2, 0, rows - kh) - r_idx
    classes, cls_of_row = np.unique(delta, return_inverse=True)
    c_idx = np.arange(GRID_W)
    cs = np.clip(c_idx - kw // 2, 0, GRID_W - kw)
    kcol = np.arange(GRID_W)
    valid = (kcol[None, :] >= cs[:, None]) & (kcol[None, :] < cs[:, None] + kw)
    pad = GRID_W
    rp = jnp.pad(rpb.astype(F32), ((0, 0), (0, 0), (pad, pad)))
    toep = jnp.stack([rp[:, :, pad + WIN_W - 1 - c:pad + WIN_W - 1 - c + GRID_W] for c in range(GRID_W)], axis=2)
    toep = jnp.where(valid[None, None, :, :], toep * LOG2_E, NEG_BIG)
    tab = jnp.stack([toep[:, int(dl) + WIN_H - 1:int(dl) + WIN_H - 1 + kh] for dl in classes], axis=0)
    tab = tab.transpose(0, 1, 3, 2, 4).reshape(len(classes), rpb.shape[0], GRID_W, kh * GRID_W)
    return tab, jnp.asarray(cls_of_row, jnp.int32), kh


def _na_attention(q, k, v, kc, vc, rpb):
    b, t, d = q.shape
    c = kc.shape[1]
    hd = NA_HEAD
    hb = d // hd
    rows = t // GRID_W
    tab, cls_of_row, kh = _na_tables(rpb, rows)
    kern = functools.partial(_na_kernel, hb=hb, hd=hd, width=GRID_W, kh=kh, rows=rows)
    grid_spec = pltpu.PrefetchScalarGridSpec(
        num_scalar_prefetch=1,
        grid=(b, d // (hb * hd), rows),
        in_specs=[pl.BlockSpec((1, GRID_W, hb * hd), lambda b_, g, r, cls: (b_, r, g)),
                  pl.BlockSpec((1, t, hb * hd), lambda b_, g, r, cls: (b_, 0, g)),
                  pl.BlockSpec((1, t, hb * hd), lambda b_, g, r, cls: (b_, 0, g)),
                  pl.BlockSpec((1, c, hb * hd), lambda b_, g, r, cls: (b_, 0, g)),
                  pl.BlockSpec((1, c, hb * hd), lambda b_, g, r, cls: (b_, 0, g)),
                  pl.BlockSpec((1, hb, GRID_W, kh * GRID_W), lambda b_, g, r, cls: (cls[r], g, 0, 0))],
        out_specs=pl.BlockSpec((1, GRID_W, hb * hd), lambda b_, g, r, cls: (b_, r, g)),
    )
    return pl.pallas_call(
        kern,
        grid_spec=grid_spec,
        out_shape=jax.ShapeDtypeStruct((b, t, d), BF16),
        compiler_params=_params("parallel", "parallel", "arbitrary"),
        name="na_attention",
    )(cls_of_row, q, k, v, kc, vc, tab)


def _rw_feat_kernel(x_ref, xp_ref, xn_ref, mod_ref, g_ref, mu_ref, wr_ref, wk_ref, wv_ref,
                    w1_ref, w2_ref, a1_ref, a2_ref, g1_ref, g2_ref, w0_ref, a0_ref,
                    r_ref, k_ref, v_ref, lw_ref, a_ref, gate_ref, *, tm, lora_w, lora_a):
    i = pl.program_id(1)
    g = g_ref[...]
    a = _norm_mod(x_ref[0], g, mod_ref, 0)
    a_prev = _norm_mod(xp_ref[0], g, mod_ref, 0)[7:8, :]
    a_next = _norm_mod(xn_ref[0], g, mod_ref, 0)[0:1, :]
    a_prev = jnp.where(i == 0, 0.0, a_prev)
    a_next = jnp.where(i == pl.num_programs(1) - 1, 0.0, a_next)
    row = lax.broadcasted_iota(jnp.int32, (tm, 1), 0)
    prev = jnp.where(row == 0, a_prev, pltpu.roll(a, 1, 0))
    nxt = jnp.where(row == tm - 1, a_next, pltpu.roll(a, tm - 1, 0))
    xx = 0.5 * (prev + nxt) - a

    def mix(j):
        return (a + xx * mu_ref[j:j + 1, :]).astype(BF16)

    r_ref[0] = _dot(mix(0), wr_ref[...])
    k_ref[0] = _dot(mix(2), wk_ref[...])
    v_ref[0] = _dot(mix(3), wv_ref[...])
    tw = jnp.tanh(_dot(mix(1), w1_ref[...]))
    ta = _dot(mix(4), a1_ref[...])
    for e in range(2):
        lora = _dot(tw[:, e * lora_w:(e + 1) * lora_w].astype(BF16), w2_ref[e])
        lw_ref[e, 0] = -EXP_M05 * _sigmoid(w0_ref[e:e + 1, :] + lora)
        la = _dot(ta[:, e * lora_a:(e + 1) * lora_a].astype(BF16), a2_ref[e])
        a_ref[e, 0] = _sigmoid(a0_ref[e:e + 1, :] + la)
    gate_ref[0] = _dot(_sigmoid(_dot(mix(5), g1_ref[...])).astype(BF16), g2_ref[...])


def _rw_features(x, mod, g, p, tm):
    b, n, d = x.shape
    nb8 = n // 8
    tb = tm // 8
    lora_w = p["w2"].shape[1]
    lora_a = p["a2"].shape[1]
    kern = functools.partial(_rw_feat_kernel, tm=tm, lora_w=lora_w, lora_a=lora_a)
    full = lambda arr: pl.BlockSpec(arr.shape, lambda b_, i: (0,) * arr.ndim)
    tile = pl.BlockSpec((1, tm, d), lambda b_, i: (b_, i, 0))
    tile2 = pl.BlockSpec((2, 1, tm, d), lambda b_, i: (0, b_, i, 0))
    names = ("mu", "wr", "wk", "wv", "w1", "w2", "a1", "a2", "g1", "g2", "w0", "a0")
    return pl.pallas_call(
        kern,
        grid=(b, n // tm),
        in_specs=[tile,
                  pl.BlockSpec((1, 8, d), lambda b_, i: (b_, jnp.maximum(i * tb - 1, 0), 0)),
                  pl.BlockSpec((1, 8, d), lambda b_, i: (b_, jnp.minimum((i + 1) * tb, nb8 - 1), 0)),
                  pl.BlockSpec((1, 6, d), _mod_index(mod.shape[0])),
                  pl.BlockSpec((1, d), lambda b_, i: (0, 0))] + [full(p[nm]) for nm in names],
        out_specs=[tile, tile, tile, tile2, tile2, tile],
        out_shape=[jax.ShapeDtypeStruct((b, n, d), F32)] * 3
        + [jax.ShapeDtypeStruct((2, b, n, d), F32)] * 2 + [jax.ShapeDtypeStruct((b, n, d), F32)],
        compiler_params=_params("parallel", "parallel"),
        name="rwkv_features",
    )(x, x, x, mod, g, *[p[nm] for nm in names])


def _rw_scan_kernel(s0_ref, rf_ref, kf_ref, vf_ref, rr_ref, kr_ref, vr_ref, lwf_ref, lwr_ref, af_ref, ar_ref,
                    kkp_ref, kap_ref, rkp_ref, yf_ref, yr_ref, bonf_ref, bonr_ref, sf_ref, s_scr, *, L, H, N):
    c = pl.program_id(1)

    @pl.when(c == 0)
    def _():
        s_scr[...] = s0_ref[:, 0]

    assert L == N and 2 * N == PAIR_LANES
    P = PAIR_LANES
    D = H * N
    n_pairs = D // P
    lane = lax.broadcasted_iota(jnp.int32, (1, P), 1)
    own0 = lane < N
    blk = ((lax.broadcasted_iota(jnp.int32, (P, P), 0) < N) == (lax.broadcasted_iota(jnp.int32, (P, P), 1) < N))

    def bd(x):
        return jnp.where(blk, jnp.concatenate([x, x], axis=0), 0.0).astype(x.dtype)

    ri = lax.broadcasted_iota(jnp.int32, (L, L), 0)
    ci = lax.broadcasted_iota(jnp.int32, (L, L), 1)
    eye2 = jnp.where(lax.broadcasted_iota(jnp.int32, (L, P), 0)
                     == (lax.broadcasted_iota(jnp.int32, (L, P), 1) & (L - 1)), 1.0, 0.0).astype(F32)
    rg = lax.broadcasted_iota(jnp.int32, (2 * L, 4 * L), 0)
    cg = lax.broadcasted_iota(jnp.int32, (2 * L, 4 * L), 1) & (L - 1)
    rgl = rg & (L - 1)
    strict_rows = jnp.where(rg < L, 1, 0)
    ind, _ = _head_indicators(D, N)
    n_fact = int(math.log2(L))
    pairs = [slice(p * P, (p + 1) * P) for p in range(n_pairs)]

    dirs = []
    for rev, (r_ref, k_ref, v_ref, lw_ref, a_ref) in enumerate(
            ((rf_ref, kf_ref, vf_ref, lwf_ref, af_ref), (rr_ref, kr_ref, vr_ref, lwr_ref, ar_ref))):
        dist = (ci - ri) if rev else (ri - ci)
        incl_b = jnp.where(dist >= 0, 1.0, 0.0).astype(BF16)
        mask_g = ((cg - rgl) if rev else (rgl - cg)) >= strict_rows
        lw = lw_ref[0, 0]
        l1, l2, l3 = _split3(lw)
        cw = _dot(incl_b, l1) + _dot(incl_b, l2) + _dot(incl_b, l3)
        tot = cw[0:1, :] if rev else cw[L - 1:L, :]
        a = a_ref[0, 0]
        k = k_ref[0]
        r = r_ref[0]
        v = v_ref[0]
        kk_raw = k * kkp_ref[...]
        kd = k * (1.0 + (a - 1.0) * kap_ref[...])
        hs = _dot3(jnp.concatenate([kk_raw * kk_raw, r * kd * rkp_ref[...]], axis=0), ind)
        dirs.append(dict(
            mask_g=mask_g, a=a, v=v, kk_raw=kk_raw,
            inv_norm=jnp.minimum(lax.rsqrt(hs[:L]), 1.0 / NORM_EPS), rk_sum=hs[L:],
            e_prev=jnp.exp(cw - lw), e_neg=jnp.exp(-cw), e_rem=jnp.exp(tot - cw), e_tot=jnp.exp(tot),
            r_t=r * jnp.exp(cw), k_t=kd * jnp.exp(-cw), k_h=kd * jnp.exp(tot - cw)))

    items = [(e, p) for e in range(2) for p in range(n_pairs)]
    idx = range(len(items))
    x_ar, z_bd, bk, vb = [], [], [], []
    for e, p in items:
        d, sl = dirs[e], pairs[p]
        h0 = p * (P // N)
        spread = lambda t: jnp.where(own0, t[:, h0:h0 + 1], t[:, h0 + 1:h0 + 2])
        kk = d["kk_raw"][:, sl] * spread(d["inv_norm"])
        b = kk * d["a"][:, sl]
        v_p = d["v"][:, sl]
        bon_ref = bonr_ref if e else bonf_ref
        bon_ref[0, :, sl] = spread(d["rk_sum"]) * v_p
        x_ar.append(jnp.concatenate([-kk * d["e_prev"][:, sl], d["r_t"][:, sl]], axis=0).astype(BF16))
        z_b = (b * d["e_neg"][:, sl]).astype(BF16)
        z_k = d["k_t"][:, sl].astype(BF16)
        zero = jnp.zeros_like(z_b)
        z_bd.append(jnp.concatenate([jnp.where(own0, z_b, zero), jnp.where(own0, zero, z_b),
                                     jnp.where(own0, z_k, zero), jnp.where(own0, zero, z_k)], axis=0))
        bk.append(jnp.concatenate([b * d["e_rem"][:, sl], d["k_h"][:, sl]], axis=0).astype(BF16))
        vb.append(v_p.astype(BF16))
    gram = [jnp.where(dirs[e]["mask_g"], _dot_nt(x_ar[i], z_bd[i]), 0.0) for i, (e, p) in enumerate(items)]
    s_old = [s_scr[e, p] for e, p in items]
    xs = [_dot_nt(x_ar[i], s_old[i].astype(BF16)) for i in idx]
    vbd = [bd(vb[i]) for i in idx]
    rhs = [xs[i][:L] + _dot(gram[i][:L, P:].astype(BF16), vbd[i]) for i in idx]

    qb = [gram[i][:L, :P].astype(BF16) for i in idx]
    t_inv = [eye2 + gram[i][:L, :P] for i in idx]
    q = [_dot(qb[i], bd(qb[i])) for i in idx]
    for _ in range(n_fact - 2):
        qb = [q[i].astype(BF16) for i in idx]
        st = [_dot(jnp.concatenate([t_inv[i].astype(BF16), qb[i]], axis=0), bd(qb[i])) for i in idx]
        t_inv = [t_inv[i] + st[i][:L] for i in idx]
        q = [st[i][L:] for i in idx]
    st = [_dot(t_inv[i].astype(BF16), bd(q[i].astype(BF16))) for i in idx]
    t_inv = [t_inv[i] + st[i] for i in idx]

    u = [_dot(t_inv[i].astype(BF16), bd(rhs[i].astype(BF16))).astype(BF16) for i in idx]
    y = [xs[i][L:] + _dot(gram[i][L:, :].astype(BF16), jnp.concatenate([bd(u[i]), vbd[i]], axis=0)) for i in idx]
    uv = [jnp.concatenate([u[i], vb[i]], axis=0) for i in idx]
    s_upd = [_dot_tn(uv[i], bk[i]) for i in idx]
    for i, (e, p) in enumerate(items):
        y_ref = yr_ref if e else yf_ref
        y_ref[0, :, pairs[p]] = y[i]
        s_scr[e, p] = jnp.where(blk, s_old[i] * dirs[e]["e_tot"][:, pairs[p]] + s_upd[i], 0.0)

    @pl.when(c == pl.num_programs(1) - 1)
    def _():
        sf_ref[:, 0] = s_scr[...]


def _rw_scan(s0, r, k, v, lw, a, kkp, kap, rkp):
    b, n, d = r.shape
    L = SCAN_CHUNK
    N = RW_HEAD
    H = d // N
    nc = n // L
    kern = functools.partial(_rw_scan_kernel, L=L, H=H, N=N)
    tok_f = pl.BlockSpec((1, L, d), lambda b_, c: (b_, c, 0))
    tok_r = pl.BlockSpec((1, L, d), lambda b_, c: (b_, nc - 1 - c, 0))
    dir_f = pl.BlockSpec((1, 1, L, d), lambda b_, c: (0, b_, c, 0))
    dir_r = pl.BlockSpec((1, 1, L, d), lambda b_, c: (1, b_, nc - 1 - c, 0))
    vec = pl.BlockSpec((1, d), lambda b_, c: (0, 0))
    n_pairs = d // PAIR_LANES
    st = pl.BlockSpec((2, 1, n_pairs, PAIR_LANES, PAIR_LANES), lambda b_, c: (0, b_, 0, 0, 0))
    tok_shape = jax.ShapeDtypeStruct((b, n, d), F32)
    return pl.pallas_call(
        kern,
        grid=(b, nc),
        in_specs=[st, tok_f, tok_f, tok_f, tok_r, tok_r, tok_r, dir_f, dir_r, dir_f, dir_r, vec, vec, vec],
        out_specs=[tok_f, tok_r, tok_f, tok_r, st],
        out_shape=[tok_shape, tok_shape, tok_shape, tok_shape, jax.ShapeDtypeStruct((2, b, n_pairs, PAIR_LANES, PAIR_LANES), F32)],
        scratch_shapes=[pltpu.VMEM((2, n_pairs, PAIR_LANES, PAIR_LANES), F32)],
        compiler_params=_params("parallel", "arbitrary"),
        name="rwkv_scan",
    )(s0, r, k, v, r, k, v, lw, lw, a, a, kkp, kap, rkp)


def _rw_out_kernel(yf_ref, yr_ref, bonf_ref, bonr_ref, gate_ref, res_ref, mod_ref, lg_ref, lb_ref, wo_ref, o_ref,
                   *, H, N):
    y = yf_ref[0] + yr_ref[0]
    ind, ind_t = _head_indicators(H * N, N)
    dlt = y - _dot3(_dot3(y, ind) * (1.0 / N), ind_t)
    var = _dot3(dlt * dlt, ind) * (1.0 / N)
    yn = dlt * _dot3(lax.rsqrt(var + LNX_EPS), ind_t)
    z = (yn * lg_ref[...] + lb_ref[...] + (bonf_ref[0] + bonr_ref[0])) * gate_ref[0]
    o_ref[0] = res_ref[0] + mod_ref[0, 2:3, :] * _dot(z.astype(BF16), wo_ref[...])


def _rw_readout(ys, bons, gate, res, mod, lg, lb, wo, tm):
    b, n, d = res.shape
    N = RW_HEAD
    kern = functools.partial(_rw_out_kernel, H=d // N, N=N)
    tile = pl.BlockSpec((1, tm, d), lambda b_, i: (b_, i, 0))
    vec = pl.BlockSpec((1, d), lambda b_, i: (0, 0))
    return pl.pallas_call(
        kern,
        grid=(b, n // tm),
        in_specs=[tile, tile, tile, tile, tile, tile, pl.BlockSpec((1, 6, d), _mod_index(mod.shape[0])),
                  vec, vec, pl.BlockSpec((d, d), lambda b_, i: (0, 0))],
        out_specs=tile,
        out_shape=jax.ShapeDtypeStruct((b, n, d), F32),
        compiler_params=_params("parallel", "parallel"),
        name="rwkv_readout",
    )(*ys, *bons, gate, res, mod, lg, lb, wo)


def _rope_tables(n_tok, hd):
    t = jnp.arange(n_tok)
    rows = (t // GRID_W).astype(F32)
    cols = (t % GRID_W).astype(F32)
    d_axis = hd // 2
    inv = jnp.float32(ROPE_BASE) ** (-jnp.arange(0, d_axis, 2, dtype=F32) / d_axis)
    ang = jnp.concatenate([rows[:, None] * inv, cols[:, None] * inv], axis=-1)
    cos = jnp.repeat(jnp.cos(ang), 2, axis=-1)
    sign = jnp.tile(jnp.asarray([-1.0, 1.0], F32), hd // 2)
    sin = jnp.repeat(jnp.sin(ang), 2, axis=-1) * sign
    return cos, sin


def _tiles(n):
    return min(n, 512), min(n, 1024)


def kernel(x, c, ctx, c_ctx, mod_w, mod_b, norm_mix, norm_ffn, ff_w1, ff_w3, ff_w2, rw_mu, rw_wr, rw_wk, rw_wv, rw_wo, rw_w0, rw_w1, rw_w2, rw_a0, rw_a1, rw_a2, rw_g1, rw_g2, rw_kk, rw_ka, rw_rk, rw_lnx_g, rw_lnx_b, at_wq, at_wk, at_wv, at_wo, at_gq, at_gk, na_wqkv, na_wo, na_gq, na_gk, na_rpb):
    B, T, D = x.shape
    C = ctx.shape[1]
    depth = mod_w.shape[0]
    bf = lambda w: w.astype(BF16)

    cc = jnp.concatenate([c, c_ctx[None, :], jnp.zeros((8 - B - 1, D), F32)], axis=0)
    mods = _modulation(cc, mod_w, mod_b)
    mod_lat = mods[:, :B].reshape(depth, B, 6, D)
    mod_ctx = mods[:, B:B + 1].reshape(depth, 1, 6, D)

    tl, tl_ffn = _tiles(T)
    tc, _ = _tiles(C)
    h_lat, h_ctx = x, ctx
    for i in range(depth):
        need_ctx = i < depth - 1
        kind, j = i % 3, i // 3
        ml, mc = mod_lat[i], mod_ctx[i]
        g_mix = norm_mix[i][None, :]
        if kind == 0:
            cat = lambda w: jnp.concatenate([w[0], w[1]], axis=1)
            p = dict(mu=rw_mu[j], wr=bf(rw_wr[j]), wk=bf(rw_wk[j]), wv=bf(rw_wv[j]),
                     w1=bf(cat(rw_w1[j])), w2=bf(rw_w2[j]), a1=bf(cat(rw_a1[j])), a2=bf(rw_a2[j]),
                     g1=bf(rw_g1[j]), g2=bf(rw_g2[j]), w0=rw_w0[j], a0=rw_a0[j])
            kkp, kap, rkp = rw_kk[j][None, :], rw_ka[j][None, :], rw_rk[j].reshape(1, D)
            lg, lb, wo = rw_lnx_g[j][None, :], rw_lnx_b[j][None, :], bf(rw_wo[j])
            r_c, k_c, v_c, lw_c, a_c, gate_c = _rw_features(h_ctx, mc, g_mix, p, min(C, 256))
            r_l, k_l, v_l, lw_l, a_l, gate_l = _rw_features(h_lat, ml, g_mix, p, min(T, 256))
            s0 = jnp.zeros((2, B, D // PAIR_LANES, PAIR_LANES, PAIR_LANES), F32)
            *out_c, s_c = _rw_scan(s0, r_c, k_c, v_c, lw_c, a_c, kkp, kap, rkp)
            *out_l, _ = _rw_scan(s_c, r_l, k_l, v_l, lw_l, a_l, kkp, kap, rkp)
            h_lat = _rw_readout(out_l[:2], out_l[2:], gate_l, h_lat, ml, lg, lb, wo, min(T, 256))
            if need_ctx:
                h_ctx = _rw_readout(out_c[:2], out_c[2:], gate_c, h_ctx, mc, lg, lb, wo, min(C, 256))
        elif kind == 1:
            w = bf(jnp.concatenate([at_wq[j], at_wk[j], at_wv[j]], axis=1))
            gq, gk, wo = at_gq[j][None, :], at_gk[j][None, :], bf(at_wo[j])
            cos, sin = _rope_tables(T, AT_HEAD)
            q_l, k_l, v_l = _gqa_qkv(h_lat, ml, g_mix, w, gq, gk, cos, sin, tl, True)
            q_c, k_c, v_c = _gqa_qkv(h_ctx, mc, g_mix, w, gq, gk, cos, sin, tc, False)
            k_all = jnp.concatenate([k_l, k_c], axis=1)
            v_all = jnp.concatenate([v_l, v_c], axis=1)
            grp = q_l.shape[2] // k_l.shape[2]
            o_l = _flash(q_l, k_all, v_all, AT_HEAD, grp, 1, tl, math.gcd(T + C, 256))
            h_lat = _oproj(o_l, h_lat, ml, wo, tl)
            if need_ctx:
                o_c = _flash(q_c, k_c, v_c, AT_HEAD, grp, 1, tc, tc)
                h_ctx = _oproj(o_c, h_ctx, mc, wo, tc)
        else:
            w, wo = bf(na_wqkv[j]), bf(na_wo[j])
            gq, gk = na_gq[j][None, :], na_gk[j][None, :]
            q_l, k_l, v_l = _na_qkv(h_lat, ml, g_mix, w, gq, gk, min(T, 256))
            q_c, k_c, v_c = _na_qkv(h_ctx, mc, g_mix, w, gq, gk, min(C, 256))
            o_l = _na_attention(q_l, k_l, v_l, k_c, v_c, na_rpb[j])
            h_lat = _oproj(o_l, h_lat, ml, wo, tl)
            if need_ctx:
                o_c = _flash(q_c, k_c, v_c, NA_HEAD, 1, 2, tc, tc)
                h_ctx = _oproj(o_c, h_ctx, mc, wo, tc)
        g_ffn = norm_ffn[i][None, :]
        w1, w3, w2 = bf(ff_w1[i]), bf(ff_w3[i]), bf(ff_w2[i])
        h_lat = _ffn(h_lat, ml, g_ffn, w1, w3, w2, tl_ffn)
        if need_ctx:
            h_ctx = _ffn(h_ctx.reshape(1, B * C, D), mc, g_ffn, w1, w3, w2, min(B * C, 1024)).reshape(B, C, D)
    return h_lat
```

```python
import functools
import math

import numpy as np
import jax
import jax.numpy as jnp
from jax import lax
from jax.experimental import pallas as pl
from jax.experimental.pallas import tpu as pltpu

F32 = jnp.float32
BF16 = jnp.bfloat16

GRID_W = 64
RW_HEAD = 64
AT_HEAD = 128
AT_KV_HEADS = 2
NA_HEAD = 64
WIN_H = 8
WIN_W = 16
ROPE_BASE = 10000.0
RMS_EPS = 1e-6
LNX_EPS = 64e-5
NORM_EPS = 1e-12
NEG_BIG = -1e30
EXP_M05 = math.exp(-0.5)
LOG2_E = math.log2(math.e)

SCAN_CHUNK = 64
PAIR_LANES = 128
VMEM_LIMIT_BYTES_V7X = 48 * 1024 * 1024


def _params(*sem):
    return pltpu.CompilerParams(dimension_semantics=sem, vmem_limit_bytes=VMEM_LIMIT_BYTES_V7X)


def _sigmoid(x):
    return 1.0 / (1.0 + jnp.exp(-x))


def _rms(x, eps):
    return x * lax.rsqrt(jnp.mean(x * x, axis=-1, keepdims=True) + eps)


def _norm_mod(x, g, mod_ref, k):
    return (_rms(x, RMS_EPS) * g) * (1.0 + mod_ref[0, k + 1:k + 2, :]) + mod_ref[0, k:k + 1, :]


def _dot(a, b):
    return jnp.dot(a, b, preferred_element_type=F32)


def _dot_nt(a, b):
    return lax.dot_general(a, b, (((1,), (1,)), ((), ())), preferred_element_type=F32)


def _dot_tn(a, b):
    return lax.dot_general(a, b, (((0,), (0,)), ((), ())), preferred_element_type=F32)


def _split2(x):
    hi = x.astype(BF16)
    return hi, (x - hi.astype(F32)).astype(BF16)


def _dot2(x, m01):
    hi, lo = _split2(x)
    return _dot(hi, m01) + _dot(lo, m01)


def _head_indicators(d, n):
    ind = jnp.where(lax.broadcasted_iota(jnp.int32, (d, 128), 0) // n
                    == lax.broadcasted_iota(jnp.int32, (d, 128), 1), 1.0, 0.0).astype(BF16)
    ind_t = jnp.where(lax.broadcasted_iota(jnp.int32, (128, d), 1) // n
                      == lax.broadcasted_iota(jnp.int32, (128, d), 0), 1.0, 0.0).astype(BF16)
    return ind, ind_t


def _mod_index(bm):
    return (lambda b, *_: (b, 0, 0)) if bm > 1 else (lambda b, *_: (0, 0, 0))


def _mod_kernel(x_ref, w_ref, b_ref, o_ref):
    x = x_ref[...]
    s = x * _sigmoid(x)
    w = w_ref[0]
    sh, sm = _split2(s)
    wh, wm = _split2(w)
    o_ref[0] = _dot(sh, wh) + _dot(sh, wm) + _dot(sm, wh) + b_ref[0]


def _modulation(cc, mod_w, mod_b):
    depth, d, e = mod_w.shape
    tn = 1536
    return pl.pallas_call(
        _mod_kernel,
        grid=(depth, e // tn),
        in_specs=[pl.BlockSpec((8, d), lambda l, j: (0, 0)),
                  pl.BlockSpec((1, d, tn), lambda l, j: (l, 0, j)),
                  pl.BlockSpec((1, 1, tn), lambda l, j: (l, 0, j))],
        out_specs=pl.BlockSpec((1, 8, tn), lambda l, j: (l, 0, j)),
        out_shape=jax.ShapeDtypeStruct((depth, 8, e), F32),
        compiler_params=_params("parallel", "parallel"),
        name="modulation",
    )(cc, mod_w, mod_b.reshape(depth, 1, e))


def _ffn_kernel(x_ref, mod_ref, g_ref, w1_ref, w3_ref, w2_ref, o_ref, f_scr, acc_scr):
    j = pl.program_id(2)

    @pl.when(j == 0)
    def _():
        f_scr[...] = _norm_mod(x_ref[0], g_ref[...], mod_ref, 3).astype(BF16)
        acc_scr[...] = jnp.zeros_like(acc_scr)

    f = f_scr[...]
    h1 = _dot(f, w1_ref[...])
    h3 = _dot(f, w3_ref[...])
    hm = (h1 * _sigmoid(h1)) * h3
    acc_scr[...] += _dot(hm.astype(BF16), w2_ref[...])

    @pl.when(j == pl.num_programs(2) - 1)
    def _():
        o_ref[0] = x_ref[0] + mod_ref[0, 5:6, :] * acc_scr[...]


def _ffn(x, mod, g, w1, w3, w2, tm):
    b, n, d = x.shape
    f = w1.shape[1]
    tf = 256
    return pl.pallas_call(
        _ffn_kernel,
        grid=(b, n // tm, f // tf),
        in_specs=[pl.BlockSpec((1, tm, d), lambda b_, i, j: (b_, i, 0)),
                  pl.BlockSpec((1, 6, d), _mod_index(mod.shape[0])),
                  pl.BlockSpec((1, d), lambda b_, i, j: (0, 0)),
                  pl.BlockSpec((d, tf), lambda b_, i, j: (0, j)),
                  pl.BlockSpec((d, tf), lambda b_, i, j: (0, j)),
                  pl.BlockSpec((tf, d), lambda b_, i, j: (j, 0))],
        out_specs=pl.BlockSpec((1, tm, d), lambda b_, i, j: (b_, i, 0)),
        out_shape=jax.ShapeDtypeStruct((b, n, d), F32),
        scratch_shapes=[pltpu.VMEM((tm, d), BF16), pltpu.VMEM((tm, d), F32)],
        compiler_params=_params("parallel", "parallel", "arbitrary"),
        name="ffn",
    )(x, mod, g, w1, w3, w2)


def _oproj_kernel(x_ref, res_ref, mod_ref, w_ref, o_ref):
    o_ref[0] = res_ref[0] + mod_ref[0, 2:3, :] * _dot(x_ref[0], w_ref[...])


def _oproj(x, res, mod, w, tm):
    b, n, k = x.shape
    d = w.shape[1]
    return pl.pallas_call(
        _oproj_kernel,
        grid=(b, n // tm),
        in_specs=[pl.BlockSpec((1, tm, k), lambda b_, i: (b_, i, 0)),
                  pl.BlockSpec((1, tm, d), lambda b_, i: (b_, i, 0)),
                  pl.BlockSpec((1, 6, d), _mod_index(mod.shape[0])),
                  pl.BlockSpec((k, d), lambda b_, i: (0, 0))],
        out_specs=pl.BlockSpec((1, tm, d), lambda b_, i: (b_, i, 0)),
        out_shape=jax.ShapeDtypeStruct((b, n, d), F32),
        compiler_params=_params("parallel", "parallel"),
        name="oproj",
    )(x, res, mod, w)


def _gqa_qkv_kernel(x_ref, mod_ref, g_ref, w_ref, gq_ref, gk_ref, cos_ref, sin_ref,
                    q_ref, k_ref, v_ref, *, rope, n_q, n_kv, hd, scale):
    a = _norm_mod(x_ref[0], g_ref[...], mod_ref, 0).astype(BF16)
    qkv = _dot(a, w_ref[...])
    if rope:
        cos = cos_ref[...]
        sin = sin_ref[...]
        even = (lax.broadcasted_iota(jnp.int32, cos.shape, 1) % 2) == 0

    def head(xh, g):
        y = _rms(xh, RMS_EPS) * g
        if rope:
            partner = jnp.where(even, pltpu.roll(y, hd - 1, 1), pltpu.roll(y, 1, 1))
            y = y * cos + partner * sin
        return y

    gq = gq_ref[...]
    gk = gk_ref[...]
    for h in range(n_q):
        sl = slice(h * hd, (h + 1) * hd)
        q_ref[0, :, sl] = (head(qkv[:, sl], gq) * scale).astype(BF16)
    for h in range(n_kv):
        sl = slice(h * hd, (h + 1) * hd)
        k_ref[0, :, sl] = head(qkv[:, n_q * hd + h * hd:n_q * hd + (h + 1) * hd], gk).astype(BF16)
    v_ref[0] = qkv[:, (n_q + n_kv) * hd:].astype(BF16)


def _gqa_qkv(x, mod, g, w, gq, gk, cos, sin, tm, rope):
    b, n, d = x.shape
    hd = AT_HEAD
    n_kv = AT_KV_HEADS
    n_q = w.shape[1] // hd - 2 * n_kv
    kern = functools.partial(_gqa_qkv_kernel, rope=rope, n_q=n_q, n_kv=n_kv, hd=hd, scale=hd ** -0.5 * LOG2_E)
    tab = (lambda b_, i: (i, 0)) if rope else (lambda b_, i: (0, 0))
    return pl.pallas_call(
        kern,
        grid=(b, n // tm),
        in_specs=[pl.BlockSpec((1, tm, d), lambda b_, i: (b_, i, 0)),
                  pl.BlockSpec((1, 6, d), _mod_index(mod.shape[0])),
                  pl.BlockSpec((1, d), lambda b_, i: (0, 0)),
                  pl.BlockSpec(w.shape, lambda b_, i: (0, 0)),
                  pl.BlockSpec((1, hd), lambda b_, i: (0, 0)),
                  pl.BlockSpec((1, hd), lambda b_, i: (0, 0)),
                  pl.BlockSpec((tm, hd), tab),
                  pl.BlockSpec((tm, hd), tab)],
        out_specs=[pl.BlockSpec((1, tm, n_q * hd), lambda b_, i: (b_, i, 0)),
                   pl.BlockSpec((1, tm, n_kv * hd), lambda b_, i: (b_, i, 0)),
                   pl.BlockSpec((1, tm, n_kv * hd), lambda b_, i: (b_, i, 0))],
        out_shape=[jax.ShapeDtypeStruct((b, n, n_q * hd), BF16),
                   jax.ShapeDtypeStruct((b, n, n_kv * hd), BF16),
                   jax.ShapeDtypeStruct((b, n, n_kv * hd), BF16)],
        compiler_params=_params("parallel", "parallel"),
        name="gqa_qkv",
    )(x, mod, g, w, gq, gk, cos, sin)


def _flash_kernel(q_ref, k_ref, v_ref, o_ref, m_scr, l_scr, acc_scr, *, kvb, grp, hd):
    j = pl.program_id(3)

    @pl.when(j == 0)
    def _():
        m_scr[...] = jnp.full_like(m_scr, NEG_BIG)
        l_scr[...] = jnp.zeros_like(l_scr)
        acc_scr[...] = jnp.zeros_like(acc_scr)

    heads = [(kh, kh * grp + g) for kh in range(kvb) for g in range(grp)]
    s_l = [_dot_nt(k_ref[0, :, kh * hd:(kh + 1) * hd], q_ref[0, :, hq * hd:(hq + 1) * hd])
           for kh, hq in heads]
    p_l, alpha_l = [], []
    for (kh, hq), s in zip(heads, s_l):
        m_prev = m_scr[hq]
        m_new = jnp.maximum(m_prev, jnp.max(s, axis=0, keepdims=True))
        alpha = jnp.exp2(m_prev - m_new)
        p = jnp.exp2(s - m_new)
        l_scr[hq] = alpha * l_scr[hq] + jnp.sum(p, axis=0, keepdims=True)
        m_scr[hq] = m_new
        p_l.append(p.astype(BF16))
        alpha_l.append(alpha)
    for (kh, hq), p, alpha in zip(heads, p_l, alpha_l):
        pv = _dot_tn(v_ref[0, :, kh * hd:(kh + 1) * hd], p)
        acc_scr[hq] = alpha * acc_scr[hq] + pv

    @pl.when(j == pl.num_programs(3) - 1)
    def _():
        for hq in range(kvb * grp):
            o_ref[0, :, hq * hd:(hq + 1) * hd] = (acc_scr[hq] / l_scr[hq]).T.astype(BF16)


def _flash(q, k, v, hd, grp, kvb, tq, tk):
    b, nq, dq = q.shape
    nk, dk = k.shape[1], k.shape[2]
    nblk = dk // (kvb * hd)
    kern = functools.partial(_flash_kernel, kvb=kvb, grp=grp, hd=hd)
    nh = kvb * grp
    return pl.pallas_call(
        kern,
        grid=(b, nblk, nq // tq, nk // tk),
        in_specs=[pl.BlockSpec((1, tq, nh * hd), lambda b_, g, i, j: (b_, i, g)),
                  pl.BlockSpec((1, tk, kvb * hd), lambda b_, g, i, j: (b_, j, g)),
                  pl.BlockSpec((1, tk, kvb * hd), lambda b_, g, i, j: (b_, j, g))],
        out_specs=pl.BlockSpec((1, tq, nh * hd), lambda b_, g, i, j: (b_, i, g)),
        out_shape=jax.ShapeDtypeStruct((b, nq, dq), BF16),
        scratch_shapes=[pltpu.VMEM((nh, 1, tq), F32), pltpu.VMEM((nh, 1, tq), F32),
                        pltpu.VMEM((nh, hd, tq), F32)],
        compiler_params=_params("parallel", "parallel", "parallel", "arbitrary"),
        name="flash_attention",
    )(q, k, v)


def _na_qkv_kernel(x_ref, mod_ref, g_ref, w_ref, gq_ref, gk_ref, q_ref, k_ref, v_ref, *, nh, hd, scale):
    a = _norm_mod(x_ref[0], g_ref[...], mod_ref, 0).astype(BF16)
    qkv = _dot(a, w_ref[...])
    d = nh * hd
    ind, ind_t = _head_indicators(d, hd)

    def head_rms(x):
        return x * _dot2(lax.rsqrt(_dot2(x * x, ind) * (1.0 / hd) + RMS_EPS), ind_t)

    q_ref[0] = (head_rms(qkv[:, :d]) * (gq_ref[...] * scale)).astype(BF16)
    k_ref[0] = (head_rms(qkv[:, d:2 * d]) * gk_ref[...]).astype(BF16)
    v_ref[0] = qkv[:, 2 * d:].astype(BF16)


def _na_qkv(x, mod, g, w, gq, gk, tm):
    b, n, d = x.shape
    hd = NA_HEAD
    nh = w.shape[1] // (3 * hd)
    kern = functools.partial(_na_qkv_kernel, nh=nh, hd=hd, scale=hd ** -0.5 * LOG2_E)
    spec_o = pl.BlockSpec((1, tm, nh * hd), lambda b_, i: (b_, i, 0))
    return pl.pallas_call(
        kern,
        grid=(b, n // tm),
        in_specs=[pl.BlockSpec((1, tm, d), lambda b_, i: (b_, i, 0)),
                  pl.BlockSpec((1, 6, d), _mod_index(mod.shape[0])),
                  pl.BlockSpec((1, d), lambda b_, i: (0, 0)),
                  pl.BlockSpec(w.shape, lambda b_, i: (0, 0)),
                  pl.BlockSpec((1, nh * hd), lambda b_, i: (0, 0)),
                  pl.BlockSpec((1, nh * hd), lambda b_, i: (0, 0))],
        out_specs=[spec_o, spec_o, spec_o],
        out_shape=[jax.ShapeDtypeStruct((b, n, nh * hd), BF16)] * 3,
        compiler_params=_params("parallel", "parallel"),
        name="na_qkv",
    )(x, mod, g, w, jnp.tile(gq, (1, nh)), jnp.tile(gk, (1, nh)))


def _na_kernel(cls_ref, q_ref, k_ref, v_ref, kc_ref, vc_ref, bias_ref, o_ref, *, hb, hd, width, kh, rows):
    del cls_ref
    r = pl.program_id(2)
    rs = jnp.clip(r - kh // 2, 0, rows - kh)
    start = pl.multiple_of(rs * width, width)
    win = pl.ds(start, kh * width)
    hp = 128 // hd
    pairs = [slice(p * 128, (p + 1) * 128) for p in range(hb // hp)]
    lane = lax.broadcasted_iota(jnp.int32, (1, 128), 1)
    own = [(lane >= j * hd) & (lane < (j + 1) * hd) for j in range(hp)]
    heads = [(p, j) for p in range(len(pairs)) for j in range(hp)]
    q_l = [jnp.where(own[j], q_ref[0, :, pairs[p]], 0.0).astype(BF16) for p, j in heads]
    s_l = [_dot_nt(q_l[i], k_ref[0, win, pairs[p]]) for i, (p, j) in enumerate(heads)]
    sc_l = [_dot_nt(q_l[i], kc_ref[0, :, pairs[p]]) for i, (p, j) in enumerate(heads)]
    p_l, pc_l, l_l = [], [], []
    for i, (p, j) in enumerate(heads):
        s = s_l[i] + bias_ref[0, p * hp + j]
        sc = sc_l[i]
        m = jnp.maximum(jnp.max(s, axis=-1, keepdims=True), jnp.max(sc, axis=-1, keepdims=True))
        e = jnp.exp2(s - m)
        ec = jnp.exp2(sc - m)
        l_l.append(jnp.sum(e, axis=-1, keepdims=True) + jnp.sum(ec, axis=-1, keepdims=True))
        p_l.append(e.astype(BF16))
        pc_l.append(ec.astype(BF16))
    o_l = [_dot(p_l[i], v_ref[0, win, pairs[p]]) + _dot(pc_l[i], vc_ref[0, :, pairs[p]])
           for i, (p, j) in enumerate(heads)]
    for p in range(len(pairs)):
        o = o_l[p * hp] / l_l[p * hp]
        for j in range(1, hp):
            o = jnp.where(own[j], o_l[p * hp + j] / l_l[p * hp + j], o)
        o_ref[0, :, pairs[p]] = o.astype(BF16)


def _na_tables(rpb, rows):
    kh, kw = min(WIN_H, rows), WIN_W
    r_idx = np.arange(rows)
    delta = np.clip(r_idx - kh // 2, 0, rows - kh) - r_idx
    classes, cls_of_row = np.unique(delta, return_inverse=True)
    c_idx = np.arange(GRID_W)
    cs = np.clip(c_idx - kw // 2, 0, GRID_W - kw)
    kcol = np.arange(GRID_W)
    valid = (kcol[None, :] >= cs[:, None]) & (kcol[None, :] < cs[:, None] + kw)
    pad = GRID_W
    rp = jnp.pad(rpb.astype(F32), ((0, 0), (0, 0), (pad, pad)))
    toep = jnp.stack([rp[:, :, pad + WIN_W - 1 - c:pad + WIN_W - 1 - c + GRID_W] for c in range(GRID_W)], axis=2)
    toep = jnp.where(valid[None, None, :, :], toep * LOG2_E, NEG_BIG)
    tab = jnp.stack([toep[:, int(dl) + WIN_H - 1:int(dl) + WIN_H - 1 + kh] for dl in classes], axis=0)
    tab = tab.transpose(0, 1, 3, 2, 4).reshape(len(classes), rpb.shape[0], GRID_W, kh * GRID_W)
    return tab, jnp.asarray(cls_of_row, jnp.int32), kh


def _na_attention(q, k, v, kc, vc, rpb):
    b, t, d = q.shape
    c = kc.shape[1]
    hd = NA_HEAD
    hb = d // hd
    rows = t // GRID_W
    tab, cls_of_row, kh = _na_tables(rpb, rows)
    kern = functools.partial(_na_kernel, hb=hb, hd=hd, width=GRID_W, kh=kh, rows=rows)
    grid_spec = pltpu.PrefetchScalarGridSpec(
        num_scalar_prefetch=1,
        grid=(b, d // (hb * hd), rows),
        in_specs=[pl.BlockSpec((1, GRID_W, hb * hd), lambda b_, g, r, cls: (b_, r, g)),
                  pl.BlockSpec((1, t, hb * hd), lambda b_, g, r, cls: (b_, 0, g)),
                  pl.BlockSpec((1, t, hb * hd), lambda b_, g, r, cls: (b_, 0, g)),
                  pl.BlockSpec((1, c, hb * hd), lambda b_, g, r, cls: (b_, 0, g)),
                  pl.BlockSpec((1, c, hb * hd), lambda b_, g, r, cls: (b_, 0, g)),
                  pl.BlockSpec((1, hb, GRID_W, kh * GRID_W), lambda b_, g, r, cls: (cls[r], g, 0, 0))],
        out_specs=pl.BlockSpec((1, GRID_W, hb * hd), lambda b_, g, r, cls: (b_, r, g)),
    )
    return pl.pallas_call(
        kern,
        grid_spec=grid_spec,
        out_shape=jax.ShapeDtypeStruct((b, t, d), BF16),
        compiler_params=_params("parallel", "parallel", "arbitrary"),
        name="na_attention",
    )(cls_of_row, q, k, v, kc, vc, tab)


def _rw_feat_kernel(x_ref, xp_ref, xn_ref, mod_ref, g_ref, mu_ref, wr_ref, wk_ref, wv_ref,
                    w1_ref, w2_ref, a1_ref, a2_ref, g1_ref, g2_ref, w0_ref, a0_ref,
                    r_ref, k_ref, v_ref, lw_ref, a_ref, gate_ref, *, tm, lora_w, lora_a):
    i = pl.program_id(1)
    g = g_ref[...]
    a = _norm_mod(x_ref[0], g, mod_ref, 0)
    a_prev = _norm_mod(xp_ref[0], g, mod_ref, 0)[7:8, :]
    a_next = _norm_mod(xn_ref[0], g, mod_ref, 0)[0:1, :]
    a_prev = jnp.where(i == 0, 0.0, a_prev)
    a_next = jnp.where(i == pl.num_programs(1) - 1, 0.0, a_next)
    row = lax.broadcasted_iota(jnp.int32, (tm, 1), 0)
    prev = jnp.where(row == 0, a_prev, pltpu.roll(a, 1, 0))
    nxt = jnp.where(row == tm - 1, a_next, pltpu.roll(a, tm - 1, 0))
    xx = 0.5 * (prev + nxt) - a

    a_b = a.astype(BF16)
    xx_b = xx.astype(BF16)

    def mix(j):
        return a_b + xx_b * mu_ref[j:j + 1, :].astype(BF16)

    r_ref[0] = _dot(mix(0), wr_ref[...])
    k_ref[0] = _dot(mix(2), wk_ref[...])
    v_ref[0] = _dot(mix(3), wv_ref[...])
    tw = jnp.tanh(_dot(mix(1), w1_ref[...]))
    ta = _dot(mix(4), a1_ref[...])
    for e in range(2):
        lora = _dot(tw[:, e * lora_w:(e + 1) * lora_w].astype(BF16), w2_ref[e])
        lw_ref[e, 0] = -EXP_M05 * _sigmoid(w0_ref[e:e + 1, :] + lora)
        la = _dot(ta[:, e * lora_a:(e + 1) * lora_a].astype(BF16), a2_ref[e])
        a_ref[e, 0] = _sigmoid(a0_ref[e:e + 1, :] + la)
    gate_ref[0] = _dot(_sigmoid(_dot(mix(5), g1_ref[...])).astype(BF16), g2_ref[...]).astype(BF16)


def _rw_features(x, mod, g, p, tm):
    b, n, d = x.shape
    nb8 = n // 8
    tb = tm // 8
    lora_w = p["w2"].shape[1]
    lora_a = p["a2"].shape[1]
    kern = functools.partial(_rw_feat_kernel, tm=tm, lora_w=lora_w, lora_a=lora_a)
    full = lambda arr: pl.BlockSpec(arr.shape, lambda b_, i: (0,) * arr.ndim)
    tile = pl.BlockSpec((1, tm, d), lambda b_, i: (b_, i, 0))
    tile2 = pl.BlockSpec((2, 1, tm, d), lambda b_, i: (0, b_, i, 0))
    names = ("mu", "wr", "wk", "wv", "w1", "w2", "a1", "a2", "g1", "g2", "w0", "a0")
    return pl.pallas_call(
        kern,
        grid=(b, n // tm),
        in_specs=[tile,
                  pl.BlockSpec((1, 8, d), lambda b_, i: (b_, jnp.maximum(i * tb - 1, 0), 0)),
                  pl.BlockSpec((1, 8, d), lambda b_, i: (b_, jnp.minimum((i + 1) * tb, nb8 - 1), 0)),
                  pl.BlockSpec((1, 6, d), _mod_index(mod.shape[0])),
                  pl.BlockSpec((1, d), lambda b_, i: (0, 0))] + [full(p[nm]) for nm in names],
        out_specs=[tile, tile, tile, tile2, tile2, tile],
        out_shape=[jax.ShapeDtypeStruct((b, n, d), F32)] * 3
        + [jax.ShapeDtypeStruct((2, b, n, d), F32)] * 2 + [jax.ShapeDtypeStruct((b, n, d), BF16)],
        compiler_params=_params("parallel", "parallel"),
        name="rwkv_features",
    )(x, x, x, mod, g, *[p[nm] for nm in names])


def _rw_scan_kernel(s0_ref, rf_ref, kf_ref, vf_ref, rr_ref, kr_ref, vr_ref, lwf_ref, lwr_ref, af_ref, ar_ref,
                    kkp_ref, kap_ref, rkp_ref, yf_ref, yr_ref, bonf_ref, bonr_ref, sf_ref, s_scr, *, L, H, N):
    c = pl.program_id(1)

    @pl.when(c == 0)
    def _():
        s_scr[...] = s0_ref[:, 0]

    assert L == N and 2 * N == PAIR_LANES
    P = PAIR_LANES
    D = H * N
    n_pairs = D // P
    lane = lax.broadcasted_iota(jnp.int32, (1, P), 1)
    own0 = lane < N
    blk = ((lax.broadcasted_iota(jnp.int32, (P, P), 0) < N) == (lax.broadcasted_iota(jnp.int32, (P, P), 1) < N))

    def bd(x):
        return jnp.where(blk, jnp.concatenate([x, x], axis=0), 0.0).astype(x.dtype)

    ri = lax.broadcasted_iota(jnp.int32, (L, L), 0)
    ci = lax.broadcasted_iota(jnp.int32, (L, L), 1)
    eye2 = jnp.where(lax.broadcasted_iota(jnp.int32, (L, P), 0)
                     == (lax.broadcasted_iota(jnp.int32, (L, P), 1) & (L - 1)), 1.0, 0.0).astype(F32)
    rg = lax.broadcasted_iota(jnp.int32, (2 * L, 4 * L), 0)
    cg = lax.broadcasted_iota(jnp.int32, (2 * L, 4 * L), 1) & (L - 1)
    rgl = rg & (L - 1)
    strict_rows = jnp.where(rg < L, 1, 0)
    ind, _ = _head_indicators(D, N)
    n_fact = int(math.log2(L))
    pairs = [slice(p * P, (p + 1) * P) for p in range(n_pairs)]

    dirs = []
    for rev, (r_ref, k_ref, v_ref, lw_ref, a_ref) in enumerate(
            ((rf_ref, kf_ref, vf_ref, lwf_ref, af_ref), (rr_ref, kr_ref, vr_ref, lwr_ref, ar_ref))):
        dist = (ci - ri) if rev else (ri - ci)
        incl_b = jnp.where(dist >= 0, 1.0, 0.0).astype(BF16)
        mask_g = ((cg - rgl) if rev else (rgl - cg)) >= strict_rows
        lw = lw_ref[0, 0]
        l1, l2 = _split2(lw)
        cw = _dot(incl_b, l1) + _dot(incl_b, l2)
        tot = cw[0:1, :] if rev else cw[L - 1:L, :]
        a = a_ref[0, 0]
        k = k_ref[0]
        r = r_ref[0]
        v = v_ref[0]
        kk_raw = k * kkp_ref[...]
        kd = k * (1.0 + (a - 1.0) * kap_ref[...])
        hs = _dot2(jnp.concatenate([kk_raw * kk_raw, r * kd * rkp_ref[...]], axis=0), ind)
        dirs.append(dict(
            mask_g=mask_g, a=a, v=v, kk_raw=kk_raw,
            inv_norm=jnp.minimum(lax.rsqrt(hs[:L]), 1.0 / NORM_EPS), rk_sum=hs[L:],
            e_prev=jnp.exp(cw - lw), e_neg=jnp.exp(-cw), e_rem=jnp.exp(tot - cw), e_tot=jnp.exp(tot),
            r_t=r * jnp.exp(cw), k_t=kd * jnp.exp(-cw), k_h=kd * jnp.exp(tot - cw)))

    items = [(e, p) for e in range(2) for p in range(n_pairs)]
    idx = range(len(items))
    x_ar, z_bd, bk, vb = [], [], [], []
    for e, p in items:
        d, sl = dirs[e], pairs[p]
        h0 = p * (P // N)
        spread = lambda t: jnp.where(own0, t[:, h0:h0 + 1], t[:, h0 + 1:h0 + 2])
        kk = d["kk_raw"][:, sl] * spread(d["inv_norm"])
        b = kk * d["a"][:, sl]
        v_p = d["v"][:, sl]
        bon_ref = bonr_ref if e else bonf_ref
        bon_ref[0, :, sl] = (spread(d["rk_sum"]) * v_p).astype(BF16)
        x_ar.append(jnp.concatenate([-kk * d["e_prev"][:, sl], d["r_t"][:, sl]], axis=0).astype(BF16))
        z_b = (b * d["e_neg"][:, sl]).astype(BF16)
        z_k = d["k_t"][:, sl].astype(BF16)
        zero = jnp.zeros_like(z_b)
        z_bd.append(jnp.concatenate([jnp.where(own0, z_b, zero), jnp.where(own0, zero, z_b),
                                     jnp.where(own0, z_k, zero), jnp.where(own0, zero, z_k)], axis=0))
        bk.append(jnp.concatenate([b * d["e_rem"][:, sl], d["k_h"][:, sl]], axis=0).astype(BF16))
        vb.append(v_p.astype(BF16))
    gram = [jnp.where(dirs[e]["mask_g"], _dot_nt(x_ar[i], z_bd[i]), 0.0) for i, (e, p) in enumerate(items)]
    s_old = [s_scr[e, p] for e, p in items]
    xs = [_dot_nt(x_ar[i], s_old[i].astype(BF16)) for i in idx]
    vbd = [bd(vb[i]) for i in idx]
    rhs = [xs[i][:L] + _dot(gram[i][:L, P:].astype(BF16), vbd[i]) for i in idx]

    qb = [gram[i][:L, :P].astype(BF16) for i in idx]
    t_inv = [eye2 + gram[i][:L, :P] for i in idx]
    q = [_dot(qb[i], bd(qb[i])) for i in idx]
    for _ in range(n_fact - 2):
        qb = [q[i].astype(BF16) for i in idx]
        st = [_dot(jnp.concatenate([t_inv[i].astype(BF16), qb[i]], axis=0), bd(qb[i])) for i in idx]
        t_inv = [t_inv[i] + st[i][:L] for i in idx]
        q = [st[i][L:] for i in idx]
    st = [_dot(t_inv[i].astype(BF16), bd(q[i].astype(BF16))) for i in idx]
    t_inv = [t_inv[i] + st[i] for i in idx]

    u = [_dot(t_inv[i].astype(BF16), bd(rhs[i].astype(BF16))).astype(BF16) for i in idx]
    y = [xs[i][L:] + _dot(gram[i][L:, :].astype(BF16), jnp.concatenate([bd(u[i]), vbd[i]], axis=0)) for i in idx]
    uv = [jnp.concatenate([u[i], vb[i]], axis=0) for i in idx]
    s_upd = [_dot_tn(uv[i], bk[i]) for i in idx]
    for i, (e, p) in enumerate(items):
        y_ref = yr_ref if e else yf_ref
        y_ref[0, :, pairs[p]] = y[i].astype(BF16)
        s_scr[e, p] = jnp.where(blk, s_old[i] * dirs[e]["e_tot"][:, pairs[p]] + s_upd[i], 0.0)

    @pl.when(c == pl.num_programs(1) - 1)
    def _():
        sf_ref[:, 0] = s_scr[...]


def _rw_scan(s0, r, k, v, lw, a, kkp, kap, rkp):
    b, n, d = r.shape
    L = SCAN_CHUNK
    N = RW_HEAD
    H = d // N
    nc = n // L
    kern = functools.partial(_rw_scan_kernel, L=L, H=H, N=N)
    tok_f = pl.BlockSpec((1, L, d), lambda b_, c: (b_, c, 0))
    tok_r = pl.BlockSpec((1, L, d), lambda b_, c: (b_, nc - 1 - c, 0))
    dir_f = pl.BlockSpec((1, 1, L, d), lambda b_, c: (0, b_, c, 0))
    dir_r = pl.BlockSpec((1, 1, L, d), lambda b_, c: (1, b_, nc - 1 - c, 0))
    vec = pl.BlockSpec((1, d), lambda b_, c: (0, 0))
    n_pairs = d // PAIR_LANES
    st = pl.BlockSpec((2, 1, n_pairs, PAIR_LANES, PAIR_LANES), lambda b_, c: (0, b_, 0, 0, 0))
    tok_shape = jax.ShapeDtypeStruct((b, n, d), BF16)
    return pl.pallas_call(
        kern,
        grid=(b, nc),
        in_specs=[st, tok_f, tok_f, tok_f, tok_r, tok_r, tok_r, dir_f, dir_r, dir_f, dir_r, vec, vec, vec],
        out_specs=[tok_f, tok_r, tok_f, tok_r, st],
        out_shape=[tok_shape, tok_shape, tok_shape, tok_shape, jax.ShapeDtypeStruct((2, b, n_pairs, PAIR_LANES, PAIR_LANES), F32)],
        scratch_shapes=[pltpu.VMEM((2, n_pairs, PAIR_LANES, PAIR_LANES), F32)],
        compiler_params=_params("parallel", "arbitrary"),
        name="rwkv_scan",
    )(s0, r, k, v, r, k, v, lw, lw, a, a, kkp, kap, rkp)


def _rw_out_kernel(yf_ref, yr_ref, bonf_ref, bonr_ref, gate_ref, res_ref, mod_ref, lg_ref, lb_ref, wo_ref, o_ref,
                   *, H, N):
    y = yf_ref[0].astype(F32) + yr_ref[0].astype(F32)
    ind, ind_t = _head_indicators(H * N, N)
    dlt = y - _dot2(_dot2(y, ind) * (1.0 / N), ind_t)
    var = _dot2(dlt * dlt, ind) * (1.0 / N)
    yn = dlt * _dot2(lax.rsqrt(var + LNX_EPS), ind_t)
    bonus = bonf_ref[0].astype(F32) + bonr_ref[0].astype(F32)
    z = (yn * lg_ref[...] + lb_ref[...] + bonus) * gate_ref[0].astype(F32)
    o_ref[0] = res_ref[0] + mod_ref[0, 2:3, :] * _dot(z.astype(BF16), wo_ref[...])


def _rw_readout(ys, bons, gate, res, mod, lg, lb, wo, tm):
    b, n, d = res.shape
    N = RW_HEAD
    kern = functools.partial(_rw_out_kernel, H=d // N, N=N)
    tile = pl.BlockSpec((1, tm, d), lambda b_, i: (b_, i, 0))
    vec = pl.BlockSpec((1, d), lambda b_, i: (0, 0))
    return pl.pallas_call(
        kern,
        grid=(b, n // tm),
        in_specs=[tile, tile, tile, tile, tile, tile, pl.BlockSpec((1, 6, d), _mod_index(mod.shape[0])),
                  vec, vec, pl.BlockSpec((d, d), lambda b_, i: (0, 0))],
        out_specs=tile,
        out_shape=jax.ShapeDtypeStruct((b, n, d), F32),
        compiler_params=_params("parallel", "parallel"),
        name="rwkv_readout",
    )(*ys, *bons, gate, res, mod, lg, lb, wo)


def _rope_tables(n_tok, hd):
    t = jnp.arange(n_tok)
    rows = (t // GRID_W).astype(F32)
    cols = (t % GRID_W).astype(F32)
    d_axis = hd // 2
    inv = jnp.float32(ROPE_BASE) ** (-jnp.arange(0, d_axis, 2, dtype=F32) / d_axis)
    ang = jnp.concatenate([rows[:, None] * inv, cols[:, None] * inv], axis=-1)
    cos = jnp.repeat(jnp.cos(ang), 2, axis=-1)
    sign = jnp.tile(jnp.asarray([-1.0, 1.0], F32), hd // 2)
    sin = jnp.repeat(jnp.sin(ang), 2, axis=-1) * sign
    return cos, sin


def _tiles(n):
    return min(n, 512), min(n, 1024)


def kernel(x, c, ctx, c_ctx, mod_w, mod_b, norm_mix, norm_ffn, ff_w1, ff_w3, ff_w2, rw_mu, rw_wr, rw_wk, rw_wv, rw_wo, rw_w0, rw_w1, rw_w2, rw_a0, rw_a1, rw_a2, rw_g1, rw_g2, rw_kk, rw_ka, rw_rk, rw_lnx_g, rw_lnx_b, at_wq, at_wk, at_wv, at_wo, at_gq, at_gk, na_wqkv, na_wo, na_gq, na_gk, na_rpb):
    B, T, D = x.shape
    C = ctx.shape[1]
    depth = mod_w.shape[0]
    bf = lambda w: w.astype(BF16)

    cc = jnp.concatenate([c, c_ctx[None, :], jnp.zeros((8 - B - 1, D), F32)], axis=0)
    mods = _modulation(cc, mod_w, mod_b)
    mod_lat = mods[:, :B].reshape(depth, B, 6, D)
    mod_ctx = mods[:, B:B + 1].reshape(depth, 1, 6, D)

    tl, tl_ffn = _tiles(T)
    tc, _ = _tiles(C)
    h_lat, h_ctx = x, ctx
    for i in range(depth):
        need_ctx = i < depth - 1
        kind, j = i % 3, i // 3
        ml, mc = mod_lat[i], mod_ctx[i]
        g_mix = norm_mix[i][None, :]
        if kind == 0:
            cat = lambda w: jnp.concatenate([w[0], w[1]], axis=1)
            p = dict(mu=rw_mu[j], wr=bf(rw_wr[j]), wk=bf(rw_wk[j]), wv=bf(rw_wv[j]),
                     w1=bf(cat(rw_w1[j])), w2=bf(rw_w2[j]), a1=bf(cat(rw_a1[j])), a2=bf(rw_a2[j]),
                     g1=bf(rw_g1[j]), g2=bf(rw_g2[j]), w0=rw_w0[j], a0=rw_a0[j])
            kkp, kap, rkp = rw_kk[j][None, :], rw_ka[j][None, :], rw_rk[j].reshape(1, D)
            lg, lb, wo = rw_lnx_g[j][None, :], rw_lnx_b[j][None, :], bf(rw_wo[j])
            r_c, k_c, v_c, lw_c, a_c, gate_c = _rw_features(h_ctx, mc, g_mix, p, min(C, 256))
            r_l, k_l, v_l, lw_l, a_l, gate_l = _rw_features(h_lat, ml, g_mix, p, min(T, 256))
            s0 = jnp.zeros((2, B, D // PAIR_LANES, PAIR_LANES, PAIR_LANES), F32)
            *out_c, s_c = _rw_scan(s0, r_c, k_c, v_c, lw_c, a_c, kkp, kap, rkp)
            *out_l, _ = _rw_scan(s_c, r_l, k_l, v_l, lw_l, a_l, kkp, kap, rkp)
            h_lat = _rw_readout(out_l[:2], out_l[2:], gate_l, h_lat, ml, lg, lb, wo, min(T, 256))
            if need_ctx:
                h_ctx = _rw_readout(out_c[:2], out_c[2:], gate_c, h_ctx, mc, lg, lb, wo, min(C, 256))
        elif kind == 1:
            w = bf(jnp.concatenate([at_wq[j], at_wk[j], at_wv[j]], axis=1))
            gq, gk, wo = at_gq[j][None, :], at_gk[j][None, :], bf(at_wo[j])
            cos, sin = _rope_tables(T, AT_HEAD)
            q_l, k_l, v_l = _gqa_qkv(h_lat, ml, g_mix, w, gq, gk, cos, sin, tl, True)
            q_c, k_c, v_c = _gqa_qkv(h_ctx, mc, g_mix, w, gq, gk, cos, sin, tc, False)
            k_all = jnp.concatenate([k_l, k_c], axis=1)
            v_all = jnp.concatenate([v_l, v_c], axis=1)
            grp = q_l.shape[2] // k_l.shape[2]
            o_l = _flash(q_l, k_all, v_all, AT_HEAD, grp, 1, tl, math.gcd(T + C, 256))
            h_lat = _oproj(o_l, h_lat, ml, wo, tl)
            if need_ctx:
                o_c = _flash(q_c, k_c, v_c, AT_HEAD, grp, 1, tc, tc)
                h_ctx = _oproj(o_c, h_ctx, mc, wo, tc)
        else:
            w, wo = bf(na_wqkv[j]), bf(na_wo[j])
            gq, gk = na_gq[j][None, :], na_gk[j][None, :]
            q_l, k_l, v_l = _na_qkv(h_lat, ml, g_mix, w, gq, gk, min(T, 256))
            q_c, k_c, v_c = _na_qkv(h_ctx, mc, g_mix, w, gq, gk, min(C, 256))
            o_l = _na_attention(q_l, k_l, v_l, k_c, v_c, na_rpb[j])
            h_lat = _oproj(o_l, h_lat, ml, wo, tl)
            if need_ctx:
                o_c = _flash(q_c, k_c, v_c, NA_HEAD, 1, 2, tc, tc)
                h_ctx = _oproj(o_c, h_ctx, mc, wo, tc)
        g_ffn = norm_ffn[i][None, :]
        w1, w3, w2 = bf(ff_w1[i]), bf(ff_w3[i]), bf(ff_w2[i])
        h_lat = _ffn(h_lat, ml, g_ffn, w1, w3, w2, tl_ffn)
        if need_ctx:
            h_ctx = _ffn(h_ctx.reshape(1, B * C, D), mc, g_ffn, w1, w3, w2, min(B * C, 1024)).reshape(B, C, D)
    return h_lat
```

```python
import functools
import math

import numpy as np
import jax
import jax.numpy as jnp
from jax import lax
from jax.experimental import pallas as pl
from jax.experimental.pallas import tpu as pltpu

F32 = jnp.float32
BF16 = jnp.bfloat16

GRID_W = 64
RW_HEAD = 64
AT_HEAD = 128
AT_KV_HEADS = 2
NA_HEAD = 64
WIN_H = 8
WIN_W = 16
ROPE_BASE = 10000.0
RMS_EPS = 1e-6
LNX_EPS = 64e-5
NORM_EPS = 1e-12
NEG_BIG = -1e30
EXP_M05 = math.exp(-0.5)
LOG2_E = math.log2(math.e)

SCAN_CHUNK = 64
NA_ROWS_PER_STEP = 2
PAIR_LANES = 128
VMEM_LIMIT_BYTES_V7X = 48 * 1024 * 1024


def _params(*sem):
    return pltpu.CompilerParams(dimension_semantics=sem, vmem_limit_bytes=VMEM_LIMIT_BYTES_V7X)


def _sigmoid(x):
    return 1.0 / (1.0 + jnp.exp(-x))


def _rms(x, eps):
    return x * lax.rsqrt(jnp.mean(x * x, axis=-1, keepdims=True) + eps)


def _norm_mod(x, g, mod_ref, k):
    return (_rms(x, RMS_EPS) * g) * (1.0 + mod_ref[0, k + 1:k + 2, :]) + mod_ref[0, k:k + 1, :]


def _dot(a, b):
    return jnp.dot(a, b, preferred_element_type=F32)


def _dot_nt(a, b):
    return lax.dot_general(a, b, (((1,), (1,)), ((), ())), preferred_element_type=F32)


def _dot_tn(a, b):
    return lax.dot_general(a, b, (((0,), (0,)), ((), ())), preferred_element_type=F32)


def _split2(x):
    hi = x.astype(BF16)
    return hi, (x - hi.astype(F32)).astype(BF16)


def _dot2(x, m01):
    hi, lo = _split2(x)
    return _dot(hi, m01) + _dot(lo, m01)


def _head_indicators(d, n):
    ind = jnp.where(lax.broadcasted_iota(jnp.int32, (d, 128), 0) // n
                    == lax.broadcasted_iota(jnp.int32, (d, 128), 1), 1.0, 0.0).astype(BF16)
    ind_t = jnp.where(lax.broadcasted_iota(jnp.int32, (128, d), 1) // n
                      == lax.broadcasted_iota(jnp.int32, (128, d), 0), 1.0, 0.0).astype(BF16)
    return ind, ind_t


def _mod_index(bm):
    return (lambda b, *_: (b, 0, 0)) if bm > 1 else (lambda b, *_: (0, 0, 0))


def _mod_kernel(x_ref, w_ref, b_ref, o_ref):
    x = x_ref[...]
    s = x * _sigmoid(x)
    w = w_ref[0]
    sh, sm = _split2(s)
    wh, wm = _split2(w)
    o_ref[0] = _dot(sh, wh) + _dot(sh, wm) + _dot(sm, wh) + b_ref[0]


def _modulation(cc, mod_w, mod_b):
    depth, d, e = mod_w.shape
    tn = 1536
    return pl.pallas_call(
        _mod_kernel,
        grid=(depth, e // tn),
        in_specs=[pl.BlockSpec((8, d), lambda l, j: (0, 0)),
                  pl.BlockSpec((1, d, tn), lambda l, j: (l, 0, j)),
                  pl.BlockSpec((1, 1, tn), lambda l, j: (l, 0, j))],
        out_specs=pl.BlockSpec((1, 8, tn), lambda l, j: (l, 0, j)),
        out_shape=jax.ShapeDtypeStruct((depth, 8, e), F32),
        compiler_params=_params("parallel", "parallel"),
        name="modulation",
    )(cc, mod_w, mod_b.reshape(depth, 1, e))


def _ffn_kernel(x_ref, mod_ref, g_ref, w1_ref, w3_ref, w2_ref, o_ref, f_scr, acc_scr):
    j = pl.program_id(2)

    @pl.when(j == 0)
    def _():
        f_scr[...] = _norm_mod(x_ref[0], g_ref[...], mod_ref, 3).astype(BF16)
        acc_scr[...] = jnp.zeros_like(acc_scr)

    f = f_scr[...]
    h1 = _dot(f, w1_ref[...])
    h3 = _dot(f, w3_ref[...])
    hm = (h1 * _sigmoid(h1)) * h3
    acc_scr[...] += _dot(hm.astype(BF16), w2_ref[...])

    @pl.when(j == pl.num_programs(2) - 1)
    def _():
        o_ref[0] = x_ref[0] + mod_ref[0, 5:6, :] * acc_scr[...]


def _ffn(x, mod, g, w1, w3, w2, tm):
    b, n, d = x.shape
    f = w1.shape[1]
    tf = 256
    return pl.pallas_call(
        _ffn_kernel,
        grid=(b, n // tm, f // tf),
        in_specs=[pl.BlockSpec((1, tm, d), lambda b_, i, j: (b_, i, 0)),
                  pl.BlockSpec((1, 6, d), _mod_index(mod.shape[0])),
                  pl.BlockSpec((1, d), lambda b_, i, j: (0, 0)),
                  pl.BlockSpec((d, tf), lambda b_, i, j: (0, j)),
                  pl.BlockSpec((d, tf), lambda b_, i, j: (0, j)),
                  pl.BlockSpec((tf, d), lambda b_, i, j: (j, 0))],
        out_specs=pl.BlockSpec((1, tm, d), lambda b_, i, j: (b_, i, 0)),
        out_shape=jax.ShapeDtypeStruct((b, n, d), F32),
        scratch_shapes=[pltpu.VMEM((tm, d), BF16), pltpu.VMEM((tm, d), F32)],
        compiler_params=_params("parallel", "parallel", "arbitrary"),
        name="ffn",
    )(x, mod, g, w1, w3, w2)


def _oproj_kernel(x_ref, res_ref, mod_ref, w_ref, o_ref):
    o_ref[0] = res_ref[0] + mod_ref[0, 2:3, :] * _dot(x_ref[0], w_ref[...])


def _oproj(x, res, mod, w, tm):
    b, n, k = x.shape
    d = w.shape[1]
    return pl.pallas_call(
        _oproj_kernel,
        grid=(b, n // tm),
        in_specs=[pl.BlockSpec((1, tm, k), lambda b_, i: (b_, i, 0)),
                  pl.BlockSpec((1, tm, d), lambda b_, i: (b_, i, 0)),
                  pl.BlockSpec((1, 6, d), _mod_index(mod.shape[0])),
                  pl.BlockSpec((k, d), lambda b_, i: (0, 0))],
        out_specs=pl.BlockSpec((1, tm, d), lambda b_, i: (b_, i, 0)),
        out_shape=jax.ShapeDtypeStruct((b, n, d), F32),
        compiler_params=_params("parallel", "parallel"),
        name="oproj",
    )(x, res, mod, w)


def _gqa_qkv_kernel(x_ref, mod_ref, g_ref, w_ref, gq_ref, gk_ref, cos_ref, sin_ref,
                    q_ref, k_ref, v_ref, *, rope, n_q, n_kv, hd, scale):
    a = _norm_mod(x_ref[0], g_ref[...], mod_ref, 0).astype(BF16)
    qkv = _dot(a, w_ref[...])
    if rope:
        cos = cos_ref[...]
        sin = sin_ref[...]
        even = (lax.broadcasted_iota(jnp.int32, cos.shape, 1) % 2) == 0

    def head(xh, g):
        y = _rms(xh, RMS_EPS) * g
        if rope:
            partner = jnp.where(even, pltpu.roll(y, hd - 1, 1), pltpu.roll(y, 1, 1))
            y = y * cos + partner * sin
        return y

    gq = gq_ref[...]
    gk = gk_ref[...]
    for h in range(n_q):
        sl = slice(h * hd, (h + 1) * hd)
        q_ref[0, :, sl] = (head(qkv[:, sl], gq) * scale).astype(BF16)
    for h in range(n_kv):
        sl = slice(h * hd, (h + 1) * hd)
        k_ref[0, :, sl] = head(qkv[:, n_q * hd + h * hd:n_q * hd + (h + 1) * hd], gk).astype(BF16)
    v_ref[0] = qkv[:, (n_q + n_kv) * hd:].astype(BF16)


def _gqa_qkv(x, mod, g, w, gq, gk, cos, sin, tm, rope):
    b, n, d = x.shape
    hd = AT_HEAD
    n_kv = AT_KV_HEADS
    n_q = w.shape[1] // hd - 2 * n_kv
    kern = functools.partial(_gqa_qkv_kernel, rope=rope, n_q=n_q, n_kv=n_kv, hd=hd, scale=hd ** -0.5 * LOG2_E)
    tab = (lambda b_, i: (i, 0)) if rope else (lambda b_, i: (0, 0))
    return pl.pallas_call(
        kern,
        grid=(b, n // tm),
        in_specs=[pl.BlockSpec((1, tm, d), lambda b_, i: (b_, i, 0)),
                  pl.BlockSpec((1, 6, d), _mod_index(mod.shape[0])),
                  pl.BlockSpec((1, d), lambda b_, i: (0, 0)),
                  pl.BlockSpec(w.shape, lambda b_, i: (0, 0)),
                  pl.BlockSpec((1, hd), lambda b_, i: (0, 0)),
                  pl.BlockSpec((1, hd), lambda b_, i: (0, 0)),
                  pl.BlockSpec((tm, hd), tab),
                  pl.BlockSpec((tm, hd), tab)],
        out_specs=[pl.BlockSpec((1, tm, n_q * hd), lambda b_, i: (b_, i, 0)),
                   pl.BlockSpec((1, tm, n_kv * hd), lambda b_, i: (b_, i, 0)),
                   pl.BlockSpec((1, tm, n_kv * hd), lambda b_, i: (b_, i, 0))],
        out_shape=[jax.ShapeDtypeStruct((b, n, n_q * hd), BF16),
                   jax.ShapeDtypeStruct((b, n, n_kv * hd), BF16),
                   jax.ShapeDtypeStruct((b, n, n_kv * hd), BF16)],
        compiler_params=_params("parallel", "parallel"),
        name="gqa_qkv",
    )(x, mod, g, w, gq, gk, cos, sin)


def _flash_kernel(q_ref, k_ref, v_ref, o_ref, m_scr, l_scr, acc_scr, *, kvb, grp, hd):
    j = pl.program_id(3)

    @pl.when(j == 0)
    def _():
        m_scr[...] = jnp.full_like(m_scr, NEG_BIG)
        l_scr[...] = jnp.zeros_like(l_scr)
        acc_scr[...] = jnp.zeros_like(acc_scr)

    heads = [(kh, kh * grp + g) for kh in range(kvb) for g in range(grp)]
    s_l = [_dot_nt(k_ref[0, :, kh * hd:(kh + 1) * hd], q_ref[0, :, hq * hd:(hq + 1) * hd])
           for kh, hq in heads]
    p_l, alpha_l = [], []
    for (kh, hq), s in zip(heads, s_l):
        m_prev = m_scr[hq]
        m_new = jnp.maximum(m_prev, jnp.max(s, axis=0, keepdims=True))
        alpha = jnp.exp2(m_prev - m_new)
        p = jnp.exp2(s - m_new)
        l_scr[hq] = alpha * l_scr[hq] + jnp.sum(p, axis=0, keepdims=True)
        m_scr[hq] = m_new
        p_l.append(p.astype(BF16))
        alpha_l.append(alpha)
    for (kh, hq), p, alpha in zip(heads, p_l, alpha_l):
        pv = _dot_tn(v_ref[0, :, kh * hd:(kh + 1) * hd], p)
        acc_scr[hq] = alpha * acc_scr[hq] + pv

    @pl.when(j == pl.num_programs(3) - 1)
    def _():
        for hq in range(kvb * grp):
            o_ref[0, :, hq * hd:(hq + 1) * hd] = (acc_scr[hq] / l_scr[hq]).T.astype(BF16)


def _flash(q, k, v, hd, grp, kvb, tq, tk):
    b, nq, dq = q.shape
    nk, dk = k.shape[1], k.shape[2]
    nblk = dk // (kvb * hd)
    kern = functools.partial(_flash_kernel, kvb=kvb, grp=grp, hd=hd)
    nh = kvb * grp
    return pl.pallas_call(
        kern,
        grid=(b, nblk, nq // tq, nk // tk),
        in_specs=[pl.BlockSpec((1, tq, nh * hd), lambda b_, g, i, j: (b_, i, g)),
                  pl.BlockSpec((1, tk, kvb * hd), lambda b_, g, i, j: (b_, j, g)),
                  pl.BlockSpec((1, tk, kvb * hd), lambda b_, g, i, j: (b_, j, g))],
        out_specs=pl.BlockSpec((1, tq, nh * hd), lambda b_, g, i, j: (b_, i, g)),
        out_shape=jax.ShapeDtypeStruct((b, nq, dq), BF16),
        scratch_shapes=[pltpu.VMEM((nh, 1, tq), F32), pltpu.VMEM((nh, 1, tq), F32),
                        pltpu.VMEM((nh, hd, tq), F32)],
        compiler_params=_params("parallel", "parallel", "parallel", "arbitrary"),
        name="flash_attention",
    )(q, k, v)


def _na_qkv_kernel(x_ref, mod_ref, g_ref, w_ref, gq_ref, gk_ref, q_ref, k_ref, v_ref, *, nh, hd, scale):
    a = _norm_mod(x_ref[0], g_ref[...], mod_ref, 0).astype(BF16)
    qkv = _dot(a, w_ref[...])
    d = nh * hd
    ind, ind_t = _head_indicators(d, hd)

    def head_rms(x):
        return x * _dot2(lax.rsqrt(_dot2(x * x, ind) * (1.0 / hd) + RMS_EPS), ind_t)

    q_ref[0] = (head_rms(qkv[:, :d]) * (gq_ref[...] * scale)).astype(BF16)
    k_ref[0] = (head_rms(qkv[:, d:2 * d]) * gk_ref[...]).astype(BF16)
    v_ref[0] = qkv[:, 2 * d:].astype(BF16)


def _na_qkv(x, mod, g, w, gq, gk, tm):
    b, n, d = x.shape
    hd = NA_HEAD
    nh = w.shape[1] // (3 * hd)
    kern = functools.partial(_na_qkv_kernel, nh=nh, hd=hd, scale=hd ** -0.5 * LOG2_E)
    spec_o = pl.BlockSpec((1, tm, nh * hd), lambda b_, i: (b_, i, 0))
    return pl.pallas_call(
        kern,
        grid=(b, n // tm),
        in_specs=[pl.BlockSpec((1, tm, d), lambda b_, i: (b_, i, 0)),
                  pl.BlockSpec((1, 6, d), _mod_index(mod.shape[0])),
                  pl.BlockSpec((1, d), lambda b_, i: (0, 0)),
                  pl.BlockSpec(w.shape, lambda b_, i: (0, 0)),
                  pl.BlockSpec((1, nh * hd), lambda b_, i: (0, 0)),
                  pl.BlockSpec((1, nh * hd), lambda b_, i: (0, 0))],
        out_specs=[spec_o, spec_o, spec_o],
        out_shape=[jax.ShapeDtypeStruct((b, n, nh * hd), BF16)] * 3,
        compiler_params=_params("parallel", "parallel"),
        name="na_qkv",
    )(x, mod, g, w, jnp.tile(gq, (1, nh)), jnp.tile(gk, (1, nh)))


def _na_kernel(cls_ref, q_ref, k_ref, v_ref, kc_ref, vc_ref, bias0_ref, bias1_ref, o_ref, *, hb, hd, width, kh, rows):
    del cls_ref
    hp = 128 // hd
    pairs = [slice(p * 128, (p + 1) * 128) for p in range(hb // hp)]
    lane = lax.broadcasted_iota(jnp.int32, (1, 128), 1)
    own = [(lane >= j * hd) & (lane < (j + 1) * hd) for j in range(hp)]
    bias_refs = (bias0_ref, bias1_ref)
    wins, qrows = [], []
    for rr in range(NA_ROWS_PER_STEP):
        r = pl.program_id(2) * NA_ROWS_PER_STEP + rr
        rs = jnp.clip(r - kh // 2, 0, rows - kh)
        wins.append(pl.ds(pl.multiple_of(rs * width, width), kh * width))
        qrows.append(slice(rr * width, (rr + 1) * width))
    heads = [(rr, p, j) for rr in range(NA_ROWS_PER_STEP) for p in range(len(pairs)) for j in range(hp)]
    q_l = [jnp.where(own[j], q_ref[0, qrows[rr], pairs[p]], 0.0).astype(BF16) for rr, p, j in heads]
    s_l = [_dot_nt(q_l[i], k_ref[0, wins[rr], pairs[p]]) for i, (rr, p, j) in enumerate(heads)]
    sc_l = [_dot_nt(q_l[i], kc_ref[0, :, pairs[p]]) for i, (rr, p, j) in enumerate(heads)]
    p_l, pc_l, l_l = [], [], []
    for i, (rr, p, j) in enumerate(heads):
        s = s_l[i] + bias_refs[rr][0, p * hp + j]
        sc = sc_l[i]
        m = jnp.maximum(jnp.max(s, axis=-1, keepdims=True), jnp.max(sc, axis=-1, keepdims=True))
        e = jnp.exp2(s - m)
        ec = jnp.exp2(sc - m)
        l_l.append(jnp.sum(e, axis=-1, keepdims=True) + jnp.sum(ec, axis=-1, keepdims=True))
        p_l.append(e.astype(BF16))
        pc_l.append(ec.astype(BF16))
    o_l = [_dot(p_l[i], v_ref[0, wins[rr], pairs[p]]) + _dot(pc_l[i], vc_ref[0, :, pairs[p]])
           for i, (rr, p, j) in enumerate(heads)]
    for rr in range(NA_ROWS_PER_STEP):
        for p in range(len(pairs)):
            base = (rr * len(pairs) + p) * hp
            o = o_l[base] / l_l[base]
            for j in range(1, hp):
                o = jnp.where(own[j], o_l[base + j] / l_l[base + j], o)
            o_ref[0, qrows[rr], pairs[p]] = o.astype(BF16)


def _na_tables(rpb, rows):
    kh, kw = min(WIN_H, rows), WIN_W
    r_idx = np.arange(rows)
    delta = np.clip(r_idx - kh // 2, 0, rows - kh) - r_idx
    classes, cls_of_row = np.unique(delta, return_inverse=True)
    c_idx = np.arange(GRID_W)
    cs = np.clip(c_idx - kw // 2, 0, GRID_W - kw)
    kcol = np.arange(GRID_W)
    valid = (kcol[None, :] >= cs[:, None]) & (kcol[None, :] < cs[:, None] + kw)
    pad = GRID_W
    rp = jnp.pad(rpb.astype(F32), ((0, 0), (0, 0), (pad, pad)))
    toep = jnp.stack([rp[:, :, pad + WIN_W - 1 - c:pad + WIN_W - 1 - c + GRID_W] for c in range(GRID_W)], axis=2)
    toep = jnp.where(valid[None, None, :, :], toep * LOG2_E, NEG_BIG)
    tab = jnp.stack([toep[:, int(dl) + WIN_H - 1:int(dl) + WIN_H - 1 + kh] for dl in classes], axis=0)
    tab = tab.transpose(0, 1, 3, 2, 4).reshape(len(classes), rpb.shape[0], GRID_W, kh * GRID_W)
    return tab, jnp.asarray(cls_of_row, jnp.int32), kh


def _na_attention(q, k, v, kc, vc, rpb):
    b, t, d = q.shape
    c = kc.shape[1]
    hd = NA_HEAD
    hb = d // hd
    rows = t // GRID_W
    tab, cls_of_row, kh = _na_tables(rpb, rows)
    kern = functools.partial(_na_kernel, hb=hb, hd=hd, width=GRID_W, kh=kh, rows=rows)
    grid_spec = pltpu.PrefetchScalarGridSpec(
        num_scalar_prefetch=1,
        grid=(b, d // (hb * hd), rows // NA_ROWS_PER_STEP),
        in_specs=[pl.BlockSpec((1, NA_ROWS_PER_STEP * GRID_W, hb * hd), lambda b_, g, r, cls: (b_, r, g)),
                  pl.BlockSpec((1, t, hb * hd), lambda b_, g, r, cls: (b_, 0, g), pipeline_mode=pl.Buffered(1)),
                  pl.BlockSpec((1, t, hb * hd), lambda b_, g, r, cls: (b_, 0, g), pipeline_mode=pl.Buffered(1)),
                  pl.BlockSpec((1, c, hb * hd), lambda b_, g, r, cls: (b_, 0, g)),
                  pl.BlockSpec((1, c, hb * hd), lambda b_, g, r, cls: (b_, 0, g)),
                  pl.BlockSpec((1, hb, GRID_W, kh * GRID_W),
                               lambda b_, g, r, cls: (cls[NA_ROWS_PER_STEP * r], g, 0, 0)),
                  pl.BlockSpec((1, hb, GRID_W, kh * GRID_W),
                               lambda b_, g, r, cls: (cls[NA_ROWS_PER_STEP * r + 1], g, 0, 0))],
        out_specs=pl.BlockSpec((1, NA_ROWS_PER_STEP * GRID_W, hb * hd), lambda b_, g, r, cls: (b_, r, g)),
    )
    return pl.pallas_call(
        kern,
        grid_spec=grid_spec,
        out_shape=jax.ShapeDtypeStruct((b, t, d), BF16),
        compiler_params=_params("parallel", "parallel", "arbitrary"),
        name="na_attention",
    )(cls_of_row, q, k, v, kc, vc, tab, tab)


def _rw_feat_kernel(x_ref, xp_ref, xn_ref, mod_ref, g_ref, mu_ref, wr_ref, wk_ref, wv_ref,
                    w1_ref, w2_ref, a1_ref, a2_ref, g1_ref, g2_ref, w0_ref, a0_ref,
                    r_ref, k_ref, v_ref, lw_ref, a_ref, gate_ref, *, tm, lora_w, lora_a):
    i = pl.program_id(1)
    g = g_ref[...]
    a = _norm_mod(x_ref[0], g, mod_ref, 0)
    a_prev = _norm_mod(xp_ref[0], g, mod_ref, 0)[7:8, :]
    a_next = _norm_mod(xn_ref[0], g, mod_ref, 0)[0:1, :]
    a_prev = jnp.where(i == 0, 0.0, a_prev)
    a_next = jnp.where(i == pl.num_programs(1) - 1, 0.0, a_next)
    row = lax.broadcasted_iota(jnp.int32, (tm, 1), 0)
    prev = jnp.where(row == 0, a_prev, pltpu.roll(a, 1, 0))
    nxt = jnp.where(row == tm - 1, a_next, pltpu.roll(a, tm - 1, 0))
    xx = 0.5 * (prev + nxt) - a

    a_b = a.astype(BF16)
    xx_b = xx.astype(BF16)

    def mix(j):
        return a_b + xx_b * mu_ref[j:j + 1, :].astype(BF16)

    r_ref[0] = _dot(mix(0), wr_ref[...])
    k_ref[0] = _dot(mix(2), wk_ref[...])
    v_ref[0] = _dot(mix(3), wv_ref[...])
    tw = jnp.tanh(_dot(mix(1), w1_ref[...]))
    ta = _dot(mix(4), a1_ref[...])
    for e in range(2):
        lora = _dot(tw[:, e * lora_w:(e + 1) * lora_w].astype(BF16), w2_ref[e])
        lw_ref[e, 0] = -EXP_M05 * _sigmoid(w0_ref[e:e + 1, :] + lora)
        la = _dot(ta[:, e * lora_a:(e + 1) * lora_a].astype(BF16), a2_ref[e])
        a_ref[e, 0] = _sigmoid(a0_ref[e:e + 1, :] + la)
    gate_ref[0] = _dot(_sigmoid(_dot(mix(5), g1_ref[...])).astype(BF16), g2_ref[...]).astype(BF16)


def _rw_features(x, mod, g, p, tm):
    b, n, d = x.shape
    nb8 = n // 8
    tb = tm // 8
    lora_w = p["w2"].shape[1]
    lora_a = p["a2"].shape[1]
    kern = functools.partial(_rw_feat_kernel, tm=tm, lora_w=lora_w, lora_a=lora_a)
    full = lambda arr: pl.BlockSpec(arr.shape, lambda b_, i: (0,) * arr.ndim)
    tile = pl.BlockSpec((1, tm, d), lambda b_, i: (b_, i, 0))
    tile2 = pl.BlockSpec((2, 1, tm, d), lambda b_, i: (0, b_, i, 0))
    names = ("mu", "wr", "wk", "wv", "w1", "w2", "a1", "a2", "g1", "g2", "w0", "a0")
    return pl.pallas_call(
        kern,
        grid=(b, n // tm),
        in_specs=[tile,
                  pl.BlockSpec((1, 8, d), lambda b_, i: (b_, jnp.maximum(i * tb - 1, 0), 0)),
                  pl.BlockSpec((1, 8, d), lambda b_, i: (b_, jnp.minimum((i + 1) * tb, nb8 - 1), 0)),
                  pl.BlockSpec((1, 6, d), _mod_index(mod.shape[0])),
                  pl.BlockSpec((1, d), lambda b_, i: (0, 0))] + [full(p[nm]) for nm in names],
        out_specs=[tile, tile, tile, tile2, tile2, tile],
        out_shape=[jax.ShapeDtypeStruct((b, n, d), F32)] * 3
        + [jax.ShapeDtypeStruct((2, b, n, d), F32)] * 2 + [jax.ShapeDtypeStruct((b, n, d), BF16)],
        compiler_params=_params("parallel", "parallel"),
        name="rwkv_features",
    )(x, x, x, mod, g, *[p[nm] for nm in names])


def _rw_scan_kernel(s0_ref, rf_ref, kf_ref, vf_ref, rr_ref, kr_ref, vr_ref, lwf_ref, lwr_ref, af_ref, ar_ref,
                    kkp_ref, kap_ref, rkp_ref, yf_ref, yr_ref, bonf_ref, bonr_ref, sf_ref, s_scr, *, L, H, N):
    c = pl.program_id(1)

    @pl.when(c == 0)
    def _():
        s_scr[...] = s0_ref[:, 0]

    assert L == N and 2 * N == PAIR_LANES
    P = PAIR_LANES
    D = H * N
    n_pairs = D // P
    lane = lax.broadcasted_iota(jnp.int32, (1, P), 1)
    own0 = lane < N
    blk = ((lax.broadcasted_iota(jnp.int32, (P, P), 0) < N) == (lax.broadcasted_iota(jnp.int32, (P, P), 1) < N))

    def bd(x):
        return jnp.where(blk, jnp.concatenate([x, x], axis=0), 0.0).astype(x.dtype)

    ri = lax.broadcasted_iota(jnp.int32, (L, L), 0)
    ci = lax.broadcasted_iota(jnp.int32, (L, L), 1)
    eye2 = jnp.where(lax.broadcasted_iota(jnp.int32, (L, P), 0)
                     == (lax.broadcasted_iota(jnp.int32, (L, P), 1) & (L - 1)), 1.0, 0.0).astype(F32)
    rg = lax.broadcasted_iota(jnp.int32, (2 * L, 4 * L), 0)
    cg = lax.broadcasted_iota(jnp.int32, (2 * L, 4 * L), 1) & (L - 1)
    rgl = rg & (L - 1)
    strict_rows = jnp.where(rg < L, 1, 0)
    ind, _ = _head_indicators(D, N)
    n_fact = int(math.log2(L))
    pairs = [slice(p * P, (p + 1) * P) for p in range(n_pairs)]

    dirs = []
    for rev, (r_ref, k_ref, v_ref, lw_ref, a_ref) in enumerate(
            ((rf_ref, kf_ref, vf_ref, lwf_ref, af_ref), (rr_ref, kr_ref, vr_ref, lwr_ref, ar_ref))):
        dist = (ci - ri) if rev else (ri - ci)
        incl_b = jnp.where(dist >= 0, 1.0, 0.0).astype(BF16)
        mask_g = ((cg - rgl) if rev else (rgl - cg)) >= strict_rows
        lw = lw_ref[0, 0]
        l1, l2 = _split2(lw)
        cw = _dot(incl_b, l1) + _dot(incl_b, l2)
        tot = cw[0:1, :] if rev else cw[L - 1:L, :]
        a = a_ref[0, 0]
        k = k_ref[0]
        r = r_ref[0]
        v = v_ref[0]
        kk_raw = k * kkp_ref[...]
        kd = k * (1.0 + (a - 1.0) * kap_ref[...])
        hs = _dot2(jnp.concatenate([kk_raw * kk_raw, r * kd * rkp_ref[...]], axis=0), ind)
        dirs.append(dict(
            mask_g=mask_g, a=a, v=v, kk_raw=kk_raw,
            inv_norm=jnp.minimum(lax.rsqrt(hs[:L]), 1.0 / NORM_EPS), rk_sum=hs[L:],
            e_prev=jnp.exp(cw - lw), e_neg=jnp.exp(-cw), e_rem=jnp.exp(tot - cw), e_tot=jnp.exp(tot),
            r_t=r * jnp.exp(cw), k_t=kd * jnp.exp(-cw), k_h=kd * jnp.exp(tot - cw)))

    items = [(e, p) for e in range(2) for p in range(n_pairs)]
    idx = range(len(items))
    x_ar, z_bd, bk, vb = [], [], [], []
    for e, p in items:
        d, sl = dirs[e], pairs[p]
        h0 = p * (P // N)
        spread = lambda t: jnp.where(own0, t[:, h0:h0 + 1], t[:, h0 + 1:h0 + 2])
        kk = d["kk_raw"][:, sl] * spread(d["inv_norm"])
        b = kk * d["a"][:, sl]
        v_p = d["v"][:, sl]
        bon_ref = bonr_ref if e else bonf_ref
        bon_ref[0, :, sl] = (spread(d["rk_sum"]) * v_p).astype(BF16)
        x_ar.append(jnp.concatenate([-kk * d["e_prev"][:, sl], d["r_t"][:, sl]], axis=0).astype(BF16))
        z_b = (b * d["e_neg"][:, sl]).astype(BF16)
        z_k = d["k_t"][:, sl].astype(BF16)
        zero = jnp.zeros_like(z_b)
        z_bd.append(jnp.concatenate([jnp.where(own0, z_b, zero), jnp.where(own0, zero, z_b),
                                     jnp.where(own0, z_k, zero), jnp.where(own0, zero, z_k)], axis=0))
        bk.append(jnp.concatenate([b * d["e_rem"][:, sl], d["k_h"][:, sl]], axis=0).astype(BF16))
        vb.append(v_p.astype(BF16))
    gram = [jnp.where(dirs[e]["mask_g"], _dot_nt(x_ar[i], z_bd[i]), 0.0) for i, (e, p) in enumerate(items)]
    s_old = [s_scr[e, p] for e, p in items]
    xs = [_dot_nt(x_ar[i], s_old[i].astype(BF16)) for i in idx]
    vbd = [bd(vb[i]) for i in idx]
    rhs = [xs[i][:L] + _dot(gram[i][:L, P:].astype(BF16), vbd[i]) for i in idx]

    qb = [gram[i][:L, :P].astype(BF16) for i in idx]
    t_inv = [eye2 + gram[i][:L, :P] for i in idx]
    q = [_dot(qb[i], bd(qb[i])) for i in idx]
    for _ in range(n_fact - 2):
        qb = [q[i].astype(BF16) for i in idx]
        st = [_dot(jnp.concatenate([t_inv[i].astype(BF16), qb[i]], axis=0), bd(qb[i])) for i in idx]
        t_inv = [t_inv[i] + st[i][:L] for i in idx]
        q = [st[i][L:] for i in idx]
    st = [_dot(t_inv[i].astype(BF16), bd(q[i].astype(BF16))) for i in idx]
    t_inv = [t_inv[i] + st[i] for i in idx]

    u = [_dot(t_inv[i].astype(BF16), bd(rhs[i].astype(BF16))).astype(BF16) for i in idx]
    y = [xs[i][L:] + _dot(gram[i][L:, :].astype(BF16), jnp.concatenate([bd(u[i]), vbd[i]], axis=0)) for i in idx]
    uv = [jnp.concatenate([u[i], vb[i]], axis=0) for i in idx]
    s_upd = [_dot_tn(uv[i], bk[i]) for i in idx]
    for i, (e, p) in enumerate(items):
        y_ref = yr_ref if e else yf_ref
        y_ref[0, :, pairs[p]] = y[i].astype(BF16)
        s_scr[e, p] = jnp.where(blk, s_old[i] * dirs[e]["e_tot"][:, pairs[p]] + s_upd[i], 0.0)

    @pl.when(c == pl.num_programs(1) - 1)
    def _():
        sf_ref[:, 0] = s_scr[...]


def _rw_scan(s0, r, k, v, lw, a, kkp, kap, rkp):
    b, n, d = r.shape
    L = SCAN_CHUNK
    N = RW_HEAD
    H = d // N
    nc = n // L
    kern = functools.partial(_rw_scan_kernel, L=L, H=H, N=N)
    tok_f = pl.BlockSpec((1, L, d), lambda b_, c: (b_, c, 0))
    tok_r = pl.BlockSpec((1, L, d), lambda b_, c: (b_, nc - 1 - c, 0))
    dir_f = pl.BlockSpec((1, 1, L, d), lambda b_, c: (0, b_, c, 0))
    dir_r = pl.BlockSpec((1, 1, L, d), lambda b_, c: (1, b_, nc - 1 - c, 0))
    vec = pl.BlockSpec((1, d), lambda b_, c: (0, 0))
    n_pairs = d // PAIR_LANES
    st = pl.BlockSpec((2, 1, n_pairs, PAIR_LANES, PAIR_LANES), lambda b_, c: (0, b_, 0, 0, 0))
    tok_shape = jax.ShapeDtypeStruct((b, n, d), BF16)
    return pl.pallas_call(
        kern,
        grid=(b, nc),
        in_specs=[st, tok_f, tok_f, tok_f, tok_r, tok_r, tok_r, dir_f, dir_r, dir_f, dir_r, vec, vec, vec],
        out_specs=[tok_f, tok_r, tok_f, tok_r, st],
        out_shape=[tok_shape, tok_shape, tok_shape, tok_shape, jax.ShapeDtypeStruct((2, b, n_pairs, PAIR_LANES, PAIR_LANES), F32)],
        scratch_shapes=[pltpu.VMEM((2, n_pairs, PAIR_LANES, PAIR_LANES), F32)],
        compiler_params=_params("parallel", "arbitrary"),
        name="rwkv_scan",
    )(s0, r, k, v, r, k, v, lw, lw, a, a, kkp, kap, rkp)


def _rw_out_kernel(yf_ref, yr_ref, bonf_ref, bonr_ref, gate_ref, res_ref, mod_ref, lg_ref, lb_ref, wo_ref, o_ref,
                   *, H, N):
    y = yf_ref[0].astype(F32) + yr_ref[0].astype(F32)
    ind, ind_t = _head_indicators(H * N, N)
    dlt = y - _dot2(_dot2(y, ind) * (1.0 / N), ind_t)
    var = _dot2(dlt * dlt, ind) * (1.0 / N)
    yn = dlt * _dot2(lax.rsqrt(var + LNX_EPS), ind_t)
    bonus = bonf_ref[0].astype(F32) + bonr_ref[0].astype(F32)
    z = (yn * lg_ref[...] + lb_ref[...] + bonus) * gate_ref[0].astype(F32)
    o_ref[0] = res_ref[0] + mod_ref[0, 2:3, :] * _dot(z.astype(BF16), wo_ref[...])


def _rw_readout(ys, bons, gate, res, mod, lg, lb, wo, tm):
    b, n, d = res.shape
    N = RW_HEAD
    kern = functools.partial(_rw_out_kernel, H=d // N, N=N)
    tile = pl.BlockSpec((1, tm, d), lambda b_, i: (b_, i, 0))
    vec = pl.BlockSpec((1, d), lambda b_, i: (0, 0))
    return pl.pallas_call(
        kern,
        grid=(b, n // tm),
        in_specs=[tile, tile, tile, tile, tile, tile, pl.BlockSpec((1, 6, d), _mod_index(mod.shape[0])),
                  vec, vec, pl.BlockSpec((d, d), lambda b_, i: (0, 0))],
        out_specs=tile,
        out_shape=jax.ShapeDtypeStruct((b, n, d), F32),
        compiler_params=_params("parallel", "parallel"),
        name="rwkv_readout",
    )(*ys, *bons, gate, res, mod, lg, lb, wo)


def _rope_tables(n_tok, hd):
    t = jnp.arange(n_tok)
    rows = (t // GRID_W).astype(F32)
    cols = (t % GRID_W).astype(F32)
    d_axis = hd // 2
    inv = jnp.float32(ROPE_BASE) ** (-jnp.arange(0, d_axis, 2, dtype=F32) / d_axis)
    ang = jnp.concatenate([rows[:, None] * inv, cols[:, None] * inv], axis=-1)
    cos = jnp.repeat(jnp.cos(ang), 2, axis=-1)
    sign = jnp.tile(jnp.asarray([-1.0, 1.0], F32), hd // 2)
    sin = jnp.repeat(jnp.sin(ang), 2, axis=-1) * sign
    return cos, sin


def _tiles(n):
    return min(n, 512), min(n, 1024)


def kernel(x, c, ctx, c_ctx, mod_w, mod_b, norm_mix, norm_ffn, ff_w1, ff_w3, ff_w2, rw_mu, rw_wr, rw_wk, rw_wv, rw_wo, rw_w0, rw_w1, rw_w2, rw_a0, rw_a1, rw_a2, rw_g1, rw_g2, rw_kk, rw_ka, rw_rk, rw_lnx_g, rw_lnx_b, at_wq, at_wk, at_wv, at_wo, at_gq, at_gk, na_wqkv, na_wo, na_gq, na_gk, na_rpb):
    B, T, D = x.shape
    C = ctx.shape[1]
    depth = mod_w.shape[0]
    bf = lambda w: w.astype(BF16)

    cc = jnp.concatenate([c, c_ctx[None, :], jnp.zeros((8 - B - 1, D), F32)], axis=0)
    mods = _modulation(cc, mod_w, mod_b)
    mod_lat = mods[:, :B].reshape(depth, B, 6, D)
    mod_ctx = mods[:, B:B + 1].reshape(depth, 1, 6, D)

    tl, tl_ffn = _tiles(T)
    tc, _ = _tiles(C)
    h_lat, h_ctx = x, ctx
    for i in range(depth):
        need_ctx = i < depth - 1
        kind, j = i % 3, i // 3
        ml, mc = mod_lat[i], mod_ctx[i]
        g_mix = norm_mix[i][None, :]
        if kind == 0:
            cat = lambda w: jnp.concatenate([w[0], w[1]], axis=1)
            p = dict(mu=rw_mu[j], wr=bf(rw_wr[j]), wk=bf(rw_wk[j]), wv=bf(rw_wv[j]),
                     w1=bf(cat(rw_w1[j])), w2=bf(rw_w2[j]), a1=bf(cat(rw_a1[j])), a2=bf(rw_a2[j]),
                     g1=bf(rw_g1[j]), g2=bf(rw_g2[j]), w0=rw_w0[j], a0=rw_a0[j])
            kkp, kap, rkp = rw_kk[j][None, :], rw_ka[j][None, :], rw_rk[j].reshape(1, D)
            lg, lb, wo = rw_lnx_g[j][None, :], rw_lnx_b[j][None, :], bf(rw_wo[j])
            r_c, k_c, v_c, lw_c, a_c, gate_c = _rw_features(h_ctx, mc, g_mix, p, min(C, 256))
            r_l, k_l, v_l, lw_l, a_l, gate_l = _rw_features(h_lat, ml, g_mix, p, min(T, 256))
            s0 = jnp.zeros((2, B, D // PAIR_LANES, PAIR_LANES, PAIR_LANES), F32)
            *out_c, s_c = _rw_scan(s0, r_c, k_c, v_c, lw_c, a_c, kkp, kap, rkp)
            *out_l, _ = _rw_scan(s_c, r_l, k_l, v_l, lw_l, a_l, kkp, kap, rkp)
            h_lat = _rw_readout(out_l[:2], out_l[2:], gate_l, h_lat, ml, lg, lb, wo, min(T, 256))
            if need_ctx:
                h_ctx = _rw_readout(out_c[:2], out_c[2:], gate_c, h_ctx, mc, lg, lb, wo, min(C, 256))
        elif kind == 1:
            w = bf(jnp.concatenate([at_wq[j], at_wk[j], at_wv[j]], axis=1))
            gq, gk, wo = at_gq[j][None, :], at_gk[j][None, :], bf(at_wo[j])
            cos, sin = _rope_tables(T, AT_HEAD)
            q_l, k_l, v_l = _gqa_qkv(h_lat, ml, g_mix, w, gq, gk, cos, sin, tl, True)
            q_c, k_c, v_c = _gqa_qkv(h_ctx, mc, g_mix, w, gq, gk, cos, sin, tc, False)
            k_all = jnp.concatenate([k_l, k_c], axis=1)
            v_all = jnp.concatenate([v_l, v_c], axis=1)
            grp = q_l.shape[2] // k_l.shape[2]
            o_l = _flash(q_l, k_all, v_all, AT_HEAD, grp, AT_KV_HEADS, tl, math.gcd(T + C, 256))
            h_lat = _oproj(o_l, h_lat, ml, wo, tl)
            if need_ctx:
                o_c = _flash(q_c, k_c, v_c, AT_HEAD, grp, 1, tc, tc)
                h_ctx = _oproj(o_c, h_ctx, mc, wo, tc)
        else:
            w, wo = bf(na_wqkv[j]), bf(na_wo[j])
            gq, gk = na_gq[j][None, :], na_gk[j][None, :]
            q_l, k_l, v_l = _na_qkv(h_lat, ml, g_mix, w, gq, gk, min(T, 256))
            q_c, k_c, v_c = _na_qkv(h_ctx, mc, g_mix, w, gq, gk, min(C, 256))
            o_l = _na_attention(q_l, k_l, v_l, k_c, v_c, na_rpb[j])
            h_lat = _oproj(o_l, h_lat, ml, wo, tl)
            if need_ctx:
                o_c = _flash(q_c, k_c, v_c, NA_HEAD, 1, 2, tc, tc)
                h_ctx = _oproj(o_c, h_ctx, mc, wo, tc)
        g_ffn = norm_ffn[i][None, :]
        w1, w3, w2 = bf(ff_w1[i]), bf(ff_w3[i]), bf(ff_w2[i])
        h_lat = _ffn(h_lat, ml, g_ffn, w1, w3, w2, tl_ffn)
        if need_ctx:
            h_ctx = _ffn(h_ctx.reshape(1, B * C, D), mc, g_ffn, w1, w3, w2, min(B * C, 1024)).reshape(B, C, D)
    return h_lat
```

```python
import functools
import math

import numpy as np
import jax
import jax.numpy as jnp
from jax import lax
from jax.experimental import pallas as pl
from jax.experimental.pallas import tpu as pltpu

F32 = jnp.float32
BF16 = jnp.bfloat16

GRID_W = 64
RW_HEAD = 64
AT_HEAD = 128
AT_KV_HEADS = 2
NA_HEAD = 64
WIN_H = 8
WIN_W = 16
ROPE_BASE = 10000.0
RMS_EPS = 1e-6
LNX_EPS = 64e-5
NORM_EPS = 1e-12
NEG_BIG = -1e30
EXP_M05 = math.exp(-0.5)
LOG2_E = math.log2(math.e)

SCAN_CHUNK = 64
NA_ROWS_PER_STEP = 2
PAIR_LANES = 128
VMEM_LIMIT_BYTES_V7X = 48 * 1024 * 1024


def _params(*sem):
    return pltpu.CompilerParams(dimension_semantics=sem, vmem_limit_bytes=VMEM_LIMIT_BYTES_V7X)


def _sigmoid(x):
    return 1.0 / (1.0 + jnp.exp(-x))


def _rms(x, eps):
    return x * lax.rsqrt(jnp.mean(x * x, axis=-1, keepdims=True) + eps)


def _norm_mod(x, g, mod_ref, k):
    return (_rms(x, RMS_EPS) * g) * (1.0 + mod_ref[0, k + 1:k + 2, :]) + mod_ref[0, k:k + 1, :]


def _dot(a, b):
    return jnp.dot(a, b, preferred_element_type=F32)


def _dot_nt(a, b):
    return lax.dot_general(a, b, (((1,), (1,)), ((), ())), preferred_element_type=F32)


def _dot_tn(a, b):
    return lax.dot_general(a, b, (((0,), (0,)), ((), ())), preferred_element_type=F32)


def _split2(x):
    hi = x.astype(BF16)
    return hi, (x - hi.astype(F32)).astype(BF16)


def _dot2(x, m01):
    hi, lo = _split2(x)
    return _dot(hi, m01) + _dot(lo, m01)


def _head_indicators(d, n):
    ind = jnp.where(lax.broadcasted_iota(jnp.int32, (d, 128), 0) // n
                    == lax.broadcasted_iota(jnp.int32, (d, 128), 1), 1.0, 0.0).astype(BF16)
    ind_t = jnp.where(lax.broadcasted_iota(jnp.int32, (128, d), 1) // n
                      == lax.broadcasted_iota(jnp.int32, (128, d), 0), 1.0, 0.0).astype(BF16)
    return ind, ind_t


def _mod_index(bm):
    return (lambda b, *_: (b, 0, 0)) if bm > 1 else (lambda b, *_: (0, 0, 0))


def _mod_kernel(x_ref, w_ref, b_ref, o_ref):
    x = x_ref[...]
    s = x * _sigmoid(x)
    w = w_ref[0]
    sh, sm = _split2(s)
    wh, wm = _split2(w)
    o_ref[0] = _dot(sh, wh) + _dot(sh, wm) + _dot(sm, wh) + b_ref[0]


def _modulation(cc, mod_w, mod_b):
    depth, d, e = mod_w.shape
    tn = 1536
    return pl.pallas_call(
        _mod_kernel,
        grid=(depth, e // tn),
        in_specs=[pl.BlockSpec((8, d), lambda l, j: (0, 0)),
                  pl.BlockSpec((1, d, tn), lambda l, j: (l, 0, j)),
                  pl.BlockSpec((1, 1, tn), lambda l, j: (l, 0, j))],
        out_specs=pl.BlockSpec((1, 8, tn), lambda l, j: (l, 0, j)),
        out_shape=jax.ShapeDtypeStruct((depth, 8, e), F32),
        compiler_params=_params("parallel", "parallel"),
        name="modulation",
    )(cc, mod_w, mod_b.reshape(depth, 1, e))


def _ffn_kernel(*refs, mixer_proj):
    if mixer_proj:
        attn_ref, wo_ref, x_ref, mod_ref, g_ref, w1_ref, w3_ref, w2_ref, o_ref, f_scr, acc_scr, h_scr = refs
    else:
        x_ref, mod_ref, g_ref, w1_ref, w3_ref, w2_ref, o_ref, f_scr, acc_scr = refs
    j = pl.program_id(2)

    @pl.when(j == 0)
    def _():
        h = x_ref[0]
        if mixer_proj:
            h = h + mod_ref[0, 2:3, :] * _dot(attn_ref[0], wo_ref[...])
            h_scr[...] = h
        f_scr[...] = _norm_mod(h, g_ref[...], mod_ref, 3).astype(BF16)
        acc_scr[...] = jnp.zeros_like(acc_scr)

    f = f_scr[...]
    h1 = _dot(f, w1_ref[...])
    h3 = _dot(f, w3_ref[...])
    hm = (h1 * _sigmoid(h1)) * h3
    acc_scr[...] += _dot(hm.astype(BF16), w2_ref[...])

    @pl.when(j == pl.num_programs(2) - 1)
    def _():
        h = h_scr[...] if mixer_proj else x_ref[0]
        o_ref[0] = h + mod_ref[0, 5:6, :] * acc_scr[...]


def _ffn(x, mod, g, w1, w3, w2, tm, attn=None, wo=None):
    b, n, d = x.shape
    f = w1.shape[1]
    tf = 256
    tile = pl.BlockSpec((1, tm, d), lambda b_, i, j: (b_, i, 0))
    in_specs = [tile,
                pl.BlockSpec((1, 6, d), _mod_index(mod.shape[0])),
                pl.BlockSpec((1, d), lambda b_, i, j: (0, 0)),
                pl.BlockSpec((d, tf), lambda b_, i, j: (0, j)),
                pl.BlockSpec((d, tf), lambda b_, i, j: (0, j)),
                pl.BlockSpec((tf, d), lambda b_, i, j: (j, 0))]
    scratch = [pltpu.VMEM((tm, d), BF16), pltpu.VMEM((tm, d), F32)]
    args = (x, mod, g, w1, w3, w2)
    if attn is not None:
        k = attn.shape[2]
        in_specs = [pl.BlockSpec((1, tm, k), lambda b_, i, j: (b_, i, 0)),
                    pl.BlockSpec((k, d), lambda b_, i, j: (0, 0))] + in_specs
        scratch = scratch + [pltpu.VMEM((tm, d), F32)]
        args = (attn, wo) + args
    return pl.pallas_call(
        functools.partial(_ffn_kernel, mixer_proj=attn is not None),
        grid=(b, n // tm, f // tf),
        in_specs=in_specs,
        out_specs=tile,
        out_shape=jax.ShapeDtypeStruct((b, n, d), F32),
        scratch_shapes=scratch,
        compiler_params=_params("parallel", "parallel", "arbitrary"),
        name="ffn",
    )(*args)


def _gqa_qkv_kernel(x_ref, mod_ref, g_ref, w_ref, gq_ref, gk_ref, cos_ref, sin_ref,
                    q_ref, k_ref, v_ref, *, rope, n_q, n_kv, hd, scale):
    a = _norm_mod(x_ref[0], g_ref[...], mod_ref, 0).astype(BF16)
    qkv = _dot(a, w_ref[...])
    if rope:
        cos = cos_ref[...]
        sin = sin_ref[...]
        even = (lax.broadcasted_iota(jnp.int32, cos.shape, 1) % 2) == 0

    def head(xh, g):
        y = _rms(xh, RMS_EPS) * g
        if rope:
            partner = jnp.where(even, pltpu.roll(y, hd - 1, 1), pltpu.roll(y, 1, 1))
            y = y * cos + partner * sin
        return y

    gq = gq_ref[...]
    gk = gk_ref[...]
    for h in range(n_q):
        sl = slice(h * hd, (h + 1) * hd)
        q_ref[0, :, sl] = (head(qkv[:, sl], gq) * scale).astype(BF16)
    for h in range(n_kv):
        sl = slice(h * hd, (h + 1) * hd)
        k_ref[0, :, sl] = head(qkv[:, n_q * hd + h * hd:n_q * hd + (h + 1) * hd], gk).astype(BF16)
    v_ref[0] = qkv[:, (n_q + n_kv) * hd:].astype(BF16)


def _gqa_qkv(x, mod, g, w, gq, gk, cos, sin, tm, rope):
    b, n, d = x.shape
    hd = AT_HEAD
    n_kv = AT_KV_HEADS
    n_q = w.shape[1] // hd - 2 * n_kv
    kern = functools.partial(_gqa_qkv_kernel, rope=rope, n_q=n_q, n_kv=n_kv, hd=hd, scale=hd ** -0.5 * LOG2_E)
    tab = (lambda b_, i: (i, 0)) if rope else (lambda b_, i: (0, 0))
    return pl.pallas_call(
        kern,
        grid=(b, n // tm),
        in_specs=[pl.BlockSpec((1, tm, d), lambda b_, i: (b_, i, 0)),
                  pl.BlockSpec((1, 6, d), _mod_index(mod.shape[0])),
                  pl.BlockSpec((1, d), lambda b_, i: (0, 0)),
                  pl.BlockSpec(w.shape, lambda b_, i: (0, 0)),
                  pl.BlockSpec((1, hd), lambda b_, i: (0, 0)),
                  pl.BlockSpec((1, hd), lambda b_, i: (0, 0)),
                  pl.BlockSpec((tm, hd), tab),
                  pl.BlockSpec((tm, hd), tab)],
        out_specs=[pl.BlockSpec((1, tm, n_q * hd), lambda b_, i: (b_, i, 0)),
                   pl.BlockSpec((1, tm, n_kv * hd), lambda b_, i: (b_, i, 0)),
                   pl.BlockSpec((1, tm, n_kv * hd), lambda b_, i: (b_, i, 0))],
        out_shape=[jax.ShapeDtypeStruct((b, n, n_q * hd), BF16),
                   jax.ShapeDtypeStruct((b, n, n_kv * hd), BF16),
                   jax.ShapeDtypeStruct((b, n, n_kv * hd), BF16)],
        compiler_params=_params("parallel", "parallel"),
        name="gqa_qkv",
    )(x, mod, g, w, gq, gk, cos, sin)


def _flash_kernel(q_ref, k_ref, v_ref, o_ref, m_scr, l_scr, acc_scr, *, kvb, grp, hd):
    j = pl.program_id(3)

    @pl.when(j == 0)
    def _():
        m_scr[...] = jnp.full_like(m_scr, NEG_BIG)
        l_scr[...] = jnp.zeros_like(l_scr)
        acc_scr[...] = jnp.zeros_like(acc_scr)

    heads = [(kh, kh * grp + g) for kh in range(kvb) for g in range(grp)]
    s_l = [_dot_nt(k_ref[0, :, kh * hd:(kh + 1) * hd], q_ref[0, :, hq * hd:(hq + 1) * hd])
           for kh, hq in heads]
    p_l, alpha_l = [], []
    for (kh, hq), s in zip(heads, s_l):
        m_prev = m_scr[hq]
        m_new = jnp.maximum(m_prev, jnp.max(s, axis=0, keepdims=True))
        alpha = jnp.exp2(m_prev - m_new)
        p = jnp.exp2(s - m_new)
        l_scr[hq] = alpha * l_scr[hq] + jnp.sum(p, axis=0, keepdims=True)
        m_scr[hq] = m_new
        p_l.append(p.astype(BF16))
        alpha_l.append(alpha)
    for (kh, hq), p, alpha in zip(heads, p_l, alpha_l):
        pv = _dot_tn(v_ref[0, :, kh * hd:(kh + 1) * hd], p)
        acc_scr[hq] = alpha * acc_scr[hq] + pv

    @pl.when(j == pl.num_programs(3) - 1)
    def _():
        for hq in range(kvb * grp):
            o_ref[0, :, hq * hd:(hq + 1) * hd] = (acc_scr[hq] / l_scr[hq]).T.astype(BF16)


def _flash(q, k, v, hd, grp, kvb, tq, tk):
    b, nq, dq = q.shape
    nk, dk = k.shape[1], k.shape[2]
    nblk = dk // (kvb * hd)
    kern = functools.partial(_flash_kernel, kvb=kvb, grp=grp, hd=hd)
    nh = kvb * grp
    return pl.pallas_call(
        kern,
        grid=(b, nblk, nq // tq, nk // tk),
        in_specs=[pl.BlockSpec((1, tq, nh * hd), lambda b_, g, i, j: (b_, i, g)),
                  pl.BlockSpec((1, tk, kvb * hd), lambda b_, g, i, j: (b_, j, g)),
                  pl.BlockSpec((1, tk, kvb * hd), lambda b_, g, i, j: (b_, j, g))],
        out_specs=pl.BlockSpec((1, tq, nh * hd), lambda b_, g, i, j: (b_, i, g)),
        out_shape=jax.ShapeDtypeStruct((b, nq, dq), BF16),
        scratch_shapes=[pltpu.VMEM((nh, 1, tq), F32), pltpu.VMEM((nh, 1, tq), F32),
                        pltpu.VMEM((nh, hd, tq), F32)],
        compiler_params=_params("parallel", "parallel", "parallel", "arbitrary"),
        name="flash_attention",
    )(q, k, v)


def _na_qkv_kernel(x_ref, mod_ref, g_ref, w_ref, gq_ref, gk_ref, q_ref, k_ref, v_ref, *, nh, hd, scale):
    a = _norm_mod(x_ref[0], g_ref[...], mod_ref, 0).astype(BF16)
    qkv = _dot(a, w_ref[...])
    d = nh * hd
    ind, ind_t = _head_indicators(d, hd)

    def head_rms(x):
        return x * _dot2(lax.rsqrt(_dot2(x * x, ind) * (1.0 / hd) + RMS_EPS), ind_t)

    q_ref[0] = (head_rms(qkv[:, :d]) * (gq_ref[...] * scale)).astype(BF16)
    k_ref[0] = (head_rms(qkv[:, d:2 * d]) * gk_ref[...]).astype(BF16)
    v_ref[0] = qkv[:, 2 * d:].astype(BF16)


def _na_qkv(x, mod, g, w, gq, gk, tm):
    b, n, d = x.shape
    hd = NA_HEAD
    nh = w.shape[1] // (3 * hd)
    kern = functools.partial(_na_qkv_kernel, nh=nh, hd=hd, scale=hd ** -0.5 * LOG2_E)
    spec_o = pl.BlockSpec((1, tm, nh * hd), lambda b_, i: (b_, i, 0))
    return pl.pallas_call(
        kern,
        grid=(b, n // tm),
        in_specs=[pl.BlockSpec((1, tm, d), lambda b_, i: (b_, i, 0)),
                  pl.BlockSpec((1, 6, d), _mod_index(mod.shape[0])),
                  pl.BlockSpec((1, d), lambda b_, i: (0, 0)),
                  pl.BlockSpec(w.shape, lambda b_, i: (0, 0)),
                  pl.BlockSpec((1, nh * hd), lambda b_, i: (0, 0)),
                  pl.BlockSpec((1, nh * hd), lambda b_, i: (0, 0))],
        out_specs=[spec_o, spec_o, spec_o],
        out_shape=[jax.ShapeDtypeStruct((b, n, nh * hd), BF16)] * 3,
        compiler_params=_params("parallel", "parallel"),
        name="na_qkv",
    )(x, mod, g, w, jnp.tile(gq, (1, nh)), jnp.tile(gk, (1, nh)))


def _na_kernel(cls_ref, q_ref, k_ref, v_ref, kc_ref, vc_ref, bias0_ref, bias1_ref, o_ref, *, hb, hd, width, kh, rows):
    del cls_ref
    hp = 128 // hd
    pairs = [slice(p * 128, (p + 1) * 128) for p in range(hb // hp)]
    lane = lax.broadcasted_iota(jnp.int32, (1, 128), 1)
    own = [(lane >= j * hd) & (lane < (j + 1) * hd) for j in range(hp)]
    bias_refs = (bias0_ref, bias1_ref)
    wins, qrows = [], []
    for rr in range(NA_ROWS_PER_STEP):
        r = pl.program_id(2) * NA_ROWS_PER_STEP + rr
        rs = jnp.clip(r - kh // 2, 0, rows - kh)
        wins.append(pl.ds(pl.multiple_of(rs * width, width), kh * width))
        qrows.append(slice(rr * width, (rr + 1) * width))
    heads = [(rr, p, j) for rr in range(NA_ROWS_PER_STEP) for p in range(len(pairs)) for j in range(hp)]
    q_l = [jnp.where(own[j], q_ref[0, qrows[rr], pairs[p]], 0.0).astype(BF16) for rr, p, j in heads]
    s_l = [_dot_nt(q_l[i], k_ref[0, wins[rr], pairs[p]]) for i, (rr, p, j) in enumerate(heads)]
    sc_l = [_dot_nt(q_l[i], kc_ref[0, :, pairs[p]]) for i, (rr, p, j) in enumerate(heads)]
    p_l, pc_l, l_l = [], [], []
    for i, (rr, p, j) in enumerate(heads):
        s = s_l[i] + bias_refs[rr][0, p * hp + j]
        sc = sc_l[i]
        m = jnp.maximum(jnp.max(s, axis=-1, keepdims=True), jnp.max(sc, axis=-1, keepdims=True))
        e = jnp.exp2(s - m)
        ec = jnp.exp2(sc - m)
        l_l.append(jnp.sum(e, axis=-1, keepdims=True) + jnp.sum(ec, axis=-1, keepdims=True))
        p_l.append(e.astype(BF16))
        pc_l.append(ec.astype(BF16))
    o_l = [_dot(p_l[i], v_ref[0, wins[rr], pairs[p]]) + _dot(pc_l[i], vc_ref[0, :, pairs[p]])
           for i, (rr, p, j) in enumerate(heads)]
    for rr in range(NA_ROWS_PER_STEP):
        for p in range(len(pairs)):
            base = (rr * len(pairs) + p) * hp
            o = o_l[base] / l_l[base]
            for j in range(1, hp):
                o = jnp.where(own[j], o_l[base + j] / l_l[base + j], o)
            o_ref[0, qrows[rr], pairs[p]] = o.astype(BF16)


def _na_tables(rpb, rows):
    kh, kw = min(WIN_H, rows), WIN_W
    r_idx = np.arange(rows)
    delta = np.clip(r_idx - kh // 2, 0, rows - kh) - r_idx
    classes, cls_of_row = np.unique(delta, return_inverse=True)
    c_idx = np.arange(GRID_W)
    cs = np.clip(c_idx - kw // 2, 0, GRID_W - kw)
    kcol = np.arange(GRID_W)
    valid = (kcol[None, :] >= cs[:, None]) & (kcol[None, :] < cs[:, None] + kw)
    pad = GRID_W
    rp = jnp.pad(rpb.astype(F32), ((0, 0), (0, 0), (pad, pad)))
    toep = jnp.stack([rp[:, :, pad + WIN_W - 1 - c:pad + WIN_W - 1 - c + GRID_W] for c in range(GRID_W)], axis=2)
    toep = jnp.where(valid[None, None, :, :], toep * LOG2_E, NEG_BIG)
    tab = jnp.stack([toep[:, int(dl) + WIN_H - 1:int(dl) + WIN_H - 1 + kh] for dl in classes], axis=0)
    tab = tab.transpose(0, 1, 3, 2, 4).reshape(len(classes), rpb.shape[0], GRID_W, kh * GRID_W)
    return tab, jnp.asarray(cls_of_row, jnp.int32), kh


def _na_attention(q, k, v, kc, vc, rpb):
    b, t, d = q.shape
    c = kc.shape[1]
    hd = NA_HEAD
    hb = d // hd
    rows = t // GRID_W
    tab, cls_of_row, kh = _na_tables(rpb, rows)
    kern = functools.partial(_na_kernel, hb=hb, hd=hd, width=GRID_W, kh=kh, rows=rows)
    grid_spec = pltpu.PrefetchScalarGridSpec(
        num_scalar_prefetch=1,
        grid=(b, d // (hb * hd), rows // NA_ROWS_PER_STEP),
        in_specs=[pl.BlockSpec((1, NA_ROWS_PER_STEP * GRID_W, hb * hd), lambda b_, g, r, cls: (b_, r, g)),
                  pl.BlockSpec((1, t, hb * hd), lambda b_, g, r, cls: (b_, 0, g), pipeline_mode=pl.Buffered(1)),
                  pl.BlockSpec((1, t, hb * hd), lambda b_, g, r, cls: (b_, 0, g), pipeline_mode=pl.Buffered(1)),
                  pl.BlockSpec((1, c, hb * hd), lambda b_, g, r, cls: (b_, 0, g)),
                  pl.BlockSpec((1, c, hb * hd), lambda b_, g, r, cls: (b_, 0, g)),
                  pl.BlockSpec((1, hb, GRID_W, kh * GRID_W),
                               lambda b_, g, r, cls: (cls[NA_ROWS_PER_STEP * r], g, 0, 0)),
                  pl.BlockSpec((1, hb, GRID_W, kh * GRID_W),
                               lambda b_, g, r, cls: (cls[NA_ROWS_PER_STEP * r + 1], g, 0, 0))],
        out_specs=pl.BlockSpec((1, NA_ROWS_PER_STEP * GRID_W, hb * hd), lambda b_, g, r, cls: (b_, r, g)),
    )
    return pl.pallas_call(
        kern,
        grid_spec=grid_spec,
        out_shape=jax.ShapeDtypeStruct((b, t, d), BF16),
        compiler_params=_params("parallel", "parallel", "arbitrary"),
        name="na_attention",
    )(cls_of_row, q, k, v, kc, vc, tab, tab)


def _rw_feat_kernel(x_ref, xp_ref, xn_ref, mod_ref, g_ref, mu_ref, wr_ref, wk_ref, wv_ref,
                    w1_ref, w2_ref, a1_ref, a2_ref, g1_ref, g2_ref, w0_ref, a0_ref,
                    r_ref, k_ref, v_ref, lw_ref, a_ref, gate_ref, *, tm, lora_w, lora_a):
    i = pl.program_id(1)
    g = g_ref[...]
    a = _norm_mod(x_ref[0], g, mod_ref, 0)
    a_prev = _norm_mod(xp_ref[0], g, mod_ref, 0)[7:8, :]
    a_next = _norm_mod(xn_ref[0], g, mod_ref, 0)[0:1, :]
    a_prev = jnp.where(i == 0, 0.0, a_prev)
    a_next = jnp.where(i == pl.num_programs(1) - 1, 0.0, a_next)
    row = lax.broadcasted_iota(jnp.int32, (tm, 1), 0)
    prev = jnp.where(row == 0, a_prev, pltpu.roll(a, 1, 0))
    nxt = jnp.where(row == tm - 1, a_next, pltpu.roll(a, tm - 1, 0))
    xx = 0.5 * (prev + nxt) - a

    a_b = a.astype(BF16)
    xx_b = xx.astype(BF16)

    def mix(j):
        return a_b + xx_b * mu_ref[j:j + 1, :].astype(BF16)

    r_ref[0] = _dot(mix(0), wr_ref[...])
    k_ref[0] = _dot(mix(2), wk_ref[...])
    v_ref[0] = _dot(mix(3), wv_ref[...])
    tw = jnp.tanh(_dot(mix(1), w1_ref[...]))
    ta = _dot(mix(4), a1_ref[...])
    for e in range(2):
        lora = _dot(tw[:, e * lora_w:(e + 1) * lora_w].astype(BF16), w2_ref[e])
        lw_ref[e, 0] = -EXP_M05 * _sigmoid(w0_ref[e:e + 1, :] + lora)
        la = _dot(ta[:, e * lora_a:(e + 1) * lora_a].astype(BF16), a2_ref[e])
        a_ref[e, 0] = _sigmoid(a0_ref[e:e + 1, :] + la)
    gate_ref[0] = _dot(_sigmoid(_dot(mix(5), g1_ref[...])).astype(BF16), g2_ref[...]).astype(BF16)


def _rw_features(x, mod, g, p, tm):
    b, n, d = x.shape
    nb8 = n // 8
    tb = tm // 8
    lora_w = p["w2"].shape[1]
    lora_a = p["a2"].shape[1]
    kern = functools.partial(_rw_feat_kernel, tm=tm, lora_w=lora_w, lora_a=lora_a)
    full = lambda arr: pl.BlockSpec(arr.shape, lambda b_, i: (0,) * arr.ndim)
    tile = pl.BlockSpec((1, tm, d), lambda b_, i: (b_, i, 0))
    tile2 = pl.BlockSpec((2, 1, tm, d), lambda b_, i: (0, b_, i, 0))
    names = ("mu", "wr", "wk", "wv", "w1", "w2", "a1", "a2", "g1", "g2", "w0", "a0")
    return pl.pallas_call(
        kern,
        grid=(b, n // tm),
        in_specs=[tile,
                  pl.BlockSpec((1, 8, d), lambda b_, i: (b_, jnp.maximum(i * tb - 1, 0), 0)),
                  pl.BlockSpec((1, 8, d), lambda b_, i: (b_, jnp.minimum((i + 1) * tb, nb8 - 1), 0)),
                  pl.BlockSpec((1, 6, d), _mod_index(mod.shape[0])),
                  pl.BlockSpec((1, d), lambda b_, i: (0, 0))] + [full(p[nm]) for nm in names],
        out_specs=[tile, tile, tile, tile2, tile2, tile],
        out_shape=[jax.ShapeDtypeStruct((b, n, d), F32)] * 3
        + [jax.ShapeDtypeStruct((2, b, n, d), F32)] * 2 + [jax.ShapeDtypeStruct((b, n, d), BF16)],
        compiler_params=_params("parallel", "parallel"),
        name="rwkv_features",
    )(x, x, x, mod, g, *[p[nm] for nm in names])


def _rw_scan_kernel(s0_ref, rf_ref, kf_ref, vf_ref, rr_ref, kr_ref, vr_ref, lwf_ref, lwr_ref, af_ref, ar_ref,
                    kkp_ref, kap_ref, rkp_ref, yf_ref, yr_ref, bonf_ref, bonr_ref, sf_ref, s_scr, *, L, H, N):
    c = pl.program_id(1)

    @pl.when(c == 0)
    def _():
        s_scr[...] = s0_ref[:, 0]

    assert L == N and 2 * N == PAIR_LANES
    P = PAIR_LANES
    D = H * N
    n_pairs = D // P
    lane = lax.broadcasted_iota(jnp.int32, (1, P), 1)
    own0 = lane < N
    blk = ((lax.broadcasted_iota(jnp.int32, (P, P), 0) < N) == (lax.broadcasted_iota(jnp.int32, (P, P), 1) < N))

    def bd(x):
        return jnp.where(blk, jnp.concatenate([x, x], axis=0), 0.0).astype(x.dtype)

    ri = lax.broadcasted_iota(jnp.int32, (L, L), 0)
    ci = lax.broadcasted_iota(jnp.int32, (L, L), 1)
    eye2 = jnp.where(lax.broadcasted_iota(jnp.int32, (L, P), 0)
                     == (lax.broadcasted_iota(jnp.int32, (L, P), 1) & (L - 1)), 1.0, 0.0).astype(F32)
    rg = lax.broadcasted_iota(jnp.int32, (2 * L, 4 * L), 0)
    cg = lax.broadcasted_iota(jnp.int32, (2 * L, 4 * L), 1) & (L - 1)
    rgl = rg & (L - 1)
    strict_rows = jnp.where(rg < L, 1, 0)
    ind, _ = _head_indicators(D, N)
    n_fact = int(math.log2(L))
    pairs = [slice(p * P, (p + 1) * P) for p in range(n_pairs)]

    dirs = []
    for rev, (r_ref, k_ref, v_ref, lw_ref, a_ref) in enumerate(
            ((rf_ref, kf_ref, vf_ref, lwf_ref, af_ref), (rr_ref, kr_ref, vr_ref, lwr_ref, ar_ref))):
        dist = (ci - ri) if rev else (ri - ci)
        incl_b = jnp.where(dist >= 0, 1.0, 0.0).astype(BF16)
        mask_g = ((cg - rgl) if rev else (rgl - cg)) >= strict_rows
        lw = lw_ref[0, 0]
        l1, l2 = _split2(lw)
        cw = _dot(incl_b, l1) + _dot(incl_b, l2)
        tot = cw[0:1, :] if rev else cw[L - 1:L, :]
        a = a_ref[0, 0]
        k = k_ref[0]
        r = r_ref[0]
        v = v_ref[0]
        kk_raw = k * kkp_ref[...]
        kd = k * (1.0 + (a - 1.0) * kap_ref[...])
        hs = _dot2(jnp.concatenate([kk_raw * kk_raw, r * kd * rkp_ref[...]], axis=0), ind)
        dirs.append(dict(
            mask_g=mask_g, a=a, v=v, kk_raw=kk_raw,
            inv_norm=jnp.minimum(lax.rsqrt(hs[:L]), 1.0 / NORM_EPS), rk_sum=hs[L:],
            e_prev=jnp.exp(cw - lw), e_neg=jnp.exp(-cw), e_rem=jnp.exp(tot - cw), e_tot=jnp.exp(tot),
            r_t=r * jnp.exp(cw), k_t=kd * jnp.exp(-cw), k_h=kd * jnp.exp(tot - cw)))

    items = [(e, p) for e in range(2) for p in range(n_pairs)]
    idx = range(len(items))
    x_ar, z_bd, bk, vb = [], [], [], []
    for e, p in items:
        d, sl = dirs[e], pairs[p]
        h0 = p * (P // N)
        spread = lambda t: jnp.where(own0, t[:, h0:h0 + 1], t[:, h0 + 1:h0 + 2])
        kk = d["kk_raw"][:, sl] * spread(d["inv_norm"])
        b = kk * d["a"][:, sl]
        v_p = d["v"][:, sl]
        bon_ref = bonr_ref if e else bonf_ref
        bon_ref[0, :, sl] = (spread(d["rk_sum"]) * v_p).astype(BF16)
        x_ar.append(jnp.concatenate([-kk * d["e_prev"][:, sl], d["r_t"][:, sl]], axis=0).astype(BF16))
        z_b = (b * d["e_neg"][:, sl]).astype(BF16)
        z_k = d["k_t"][:, sl].astype(BF16)
        zero = jnp.zeros_like(z_b)
        z_bd.append(jnp.concatenate([jnp.where(own0, z_b, zero), jnp.where(own0, zero, z_b),
                                     jnp.where(own0, z_k, zero), jnp.where(own0, zero, z_k)], axis=0))
        bk.append(jnp.concatenate([b * d["e_rem"][:, sl], d["k_h"][:, sl]], axis=0).astype(BF16))
        vb.append(v_p.astype(BF16))
    gram = [jnp.where(dirs[e]["mask_g"], _dot_nt(x_ar[i], z_bd[i]), 0.0) for i, (e, p) in enumerate(items)]
    s_old = [s_scr[e, p] for e, p in items]
    xs = [_dot_nt(x_ar[i], s_old[i].astype(BF16)) for i in idx]
    vbd = [bd(vb[i]) for i in idx]
    rhs = [xs[i][:L] + _dot(gram[i][:L, P:].astype(BF16), vbd[i]) for i in idx]

    qb = [gram[i][:L, :P].astype(BF16) for i in idx]
    t_inv = [eye2 + gram[i][:L, :P] for i in idx]
    q = [_dot(qb[i], bd(qb[i])) for i in idx]
    for _ in range(n_fact - 2):
        qb = [q[i].astype(BF16) for i in idx]
        st = [_dot(jnp.concatenate([t_inv[i].astype(BF16), qb[i]], axis=0), bd(qb[i])) for i in idx]
        t_inv = [t_inv[i] + st[i][:L] for i in idx]
        q = [st[i][L:] for i in idx]
    st = [_dot(t_inv[i].astype(BF16), bd(q[i].astype(BF16))) for i in idx]
    t_inv = [t_inv[i] + st[i] for i in idx]

    u = [_dot(t_inv[i].astype(BF16), bd(rhs[i].astype(BF16))).astype(BF16) for i in idx]
    y = [xs[i][L:] + _dot(gram[i][L:, :].astype(BF16), jnp.concatenate([bd(u[i]), vbd[i]], axis=0)) for i in idx]
    uv = [jnp.concatenate([u[i], vb[i]], axis=0) for i in idx]
    s_upd = [_dot_tn(uv[i], bk[i]) for i in idx]
    for i, (e, p) in enumerate(items):
        y_ref = yr_ref if e else yf_ref
        y_ref[0, :, pairs[p]] = y[i].astype(BF16)
        s_scr[e, p] = jnp.where(blk, s_old[i] * dirs[e]["e_tot"][:, pairs[p]] + s_upd[i], 0.0)

    @pl.when(c == pl.num_programs(1) - 1)
    def _():
        sf_ref[:, 0] = s_scr[...]


def _rw_scan(s0, r, k, v, lw, a, kkp, kap, rkp):
    b, n, d = r.shape
    L = SCAN_CHUNK
    N = RW_HEAD
    H = d // N
    nc = n // L
    kern = functools.partial(_rw_scan_kernel, L=L, H=H, N=N)
    tok_f = pl.BlockSpec((1, L, d), lambda b_, c: (b_, c, 0))
    tok_r = pl.BlockSpec((1, L, d), lambda b_, c: (b_, nc - 1 - c, 0))
    dir_f = pl.BlockSpec((1, 1, L, d), lambda b_, c: (0, b_, c, 0))
    dir_r = pl.BlockSpec((1, 1, L, d), lambda b_, c: (1, b_, nc - 1 - c, 0))
    vec = pl.BlockSpec((1, d), lambda b_, c: (0, 0))
    n_pairs = d // PAIR_LANES
    st = pl.BlockSpec((2, 1, n_pairs, PAIR_LANES, PAIR_LANES), lambda b_, c: (0, b_, 0, 0, 0))
    tok_shape = jax.ShapeDtypeStruct((b, n, d), BF16)
    return pl.pallas_call(
        kern,
        grid=(b, nc),
        in_specs=[st, tok_f, tok_f, tok_f, tok_r, tok_r, tok_r, dir_f, dir_r, dir_f, dir_r, vec, vec, vec],
        out_specs=[tok_f, tok_r, tok_f, tok_r, st],
        out_shape=[tok_shape, tok_shape, tok_shape, tok_shape, jax.ShapeDtypeStruct((2, b, n_pairs, PAIR_LANES, PAIR_LANES), F32)],
        scratch_shapes=[pltpu.VMEM((2, n_pairs, PAIR_LANES, PAIR_LANES), F32)],
        compiler_params=_params("parallel", "arbitrary"),
        name="rwkv_scan",
    )(s0, r, k, v, r, k, v, lw, lw, a, a, kkp, kap, rkp)


def _rw_out_kernel(yf_ref, yr_ref, bonf_ref, bonr_ref, gate_ref, res_ref, mod_ref, lg_ref, lb_ref, wo_ref, o_ref,
                   *, H, N):
    y = yf_ref[0].astype(F32) + yr_ref[0].astype(F32)
    ind, ind_t = _head_indicators(H * N, N)
    dlt = y - _dot2(_dot2(y, ind) * (1.0 / N), ind_t)
    var = _dot2(dlt * dlt, ind) * (1.0 / N)
    yn = dlt * _dot2(lax.rsqrt(var + LNX_EPS), ind_t)
    bonus = bonf_ref[0].astype(F32) + bonr_ref[0].astype(F32)
    z = (yn * lg_ref[...] + lb_ref[...] + bonus) * gate_ref[0].astype(F32)
    o_ref[0] = res_ref[0] + mod_ref[0, 2:3, :] * _dot(z.astype(BF16), wo_ref[...])


def _rw_readout(ys, bons, gate, res, mod, lg, lb, wo, tm):
    b, n, d = res.shape
    N = RW_HEAD
    kern = functools.partial(_rw_out_kernel, H=d // N, N=N)
    tile = pl.BlockSpec((1, tm, d), lambda b_, i: (b_, i, 0))
    vec = pl.BlockSpec((1, d), lambda b_, i: (0, 0))
    return pl.pallas_call(
        kern,
        grid=(b, n // tm),
        in_specs=[tile, tile, tile, tile, tile, tile, pl.BlockSpec((1, 6, d), _mod_index(mod.shape[0])),
                  vec, vec, pl.BlockSpec((d, d), lambda b_, i: (0, 0))],
        out_specs=tile,
        out_shape=jax.ShapeDtypeStruct((b, n, d), F32),
        compiler_params=_params("parallel", "parallel"),
        name="rwkv_readout",
    )(*ys, *bons, gate, res, mod, lg, lb, wo)


def _rope_tables(n_tok, hd):
    t = jnp.arange(n_tok)
    rows = (t // GRID_W).astype(F32)
    cols = (t % GRID_W).astype(F32)
    d_axis = hd // 2
    inv = jnp.float32(ROPE_BASE) ** (-jnp.arange(0, d_axis, 2, dtype=F32) / d_axis)
    ang = jnp.concatenate([rows[:, None] * inv, cols[:, None] * inv], axis=-1)
    cos = jnp.repeat(jnp.cos(ang), 2, axis=-1)
    sign = jnp.tile(jnp.asarray([-1.0, 1.0], F32), hd // 2)
    sin = jnp.repeat(jnp.sin(ang), 2, axis=-1) * sign
    return cos, sin


def _tiles(n):
    return min(n, 512), min(n, 1024)


def kernel(x, c, ctx, c_ctx, mod_w, mod_b, norm_mix, norm_ffn, ff_w1, ff_w3, ff_w2, rw_mu, rw_wr, rw_wk, rw_wv, rw_wo, rw_w0, rw_w1, rw_w2, rw_a0, rw_a1, rw_a2, rw_g1, rw_g2, rw_kk, rw_ka, rw_rk, rw_lnx_g, rw_lnx_b, at_wq, at_wk, at_wv, at_wo, at_gq, at_gk, na_wqkv, na_wo, na_gq, na_gk, na_rpb):
    B, T, D = x.shape
    C = ctx.shape[1]
    depth = mod_w.shape[0]
    bf = lambda w: w.astype(BF16)

    cc = jnp.concatenate([c, c_ctx[None, :], jnp.zeros((8 - B - 1, D), F32)], axis=0)
    mods = _modulation(cc, mod_w, mod_b)
    mod_lat = mods[:, :B].reshape(depth, B, 6, D)
    mod_ctx = mods[:, B:B + 1].reshape(depth, 1, 6, D)

    tl, tl_ffn = _tiles(T)
    tc, _ = _tiles(C)
    h_lat, h_ctx = x, ctx
    for i in range(depth):
        need_ctx = i < depth - 1
        kind, j = i % 3, i // 3
        ml, mc = mod_lat[i], mod_ctx[i]
        g_mix = norm_mix[i][None, :]
        if kind == 0:
            cat = lambda w: jnp.concatenate([w[0], w[1]], axis=1)
            p = dict(mu=rw_mu[j], wr=bf(rw_wr[j]), wk=bf(rw_wk[j]), wv=bf(rw_wv[j]),
                     w1=bf(cat(rw_w1[j])), w2=bf(rw_w2[j]), a1=bf(cat(rw_a1[j])), a2=bf(rw_a2[j]),
                     g1=bf(rw_g1[j]), g2=bf(rw_g2[j]), w0=rw_w0[j], a0=rw_a0[j])
            kkp, kap, rkp = rw_kk[j][None, :], rw_ka[j][None, :], rw_rk[j].reshape(1, D)
            lg, lb, wo = rw_lnx_g[j][None, :], rw_lnx_b[j][None, :], bf(rw_wo[j])
            r_c, k_c, v_c, lw_c, a_c, gate_c = _rw_features(h_ctx, mc, g_mix, p, min(C, 256))
            r_l, k_l, v_l, lw_l, a_l, gate_l = _rw_features(h_lat, ml, g_mix, p, min(T, 256))
            s0 = jnp.zeros((2, B, D // PAIR_LANES, PAIR_LANES, PAIR_LANES), F32)
            *out_c, s_c = _rw_scan(s0, r_c, k_c, v_c, lw_c, a_c, kkp, kap, rkp)
            *out_l, _ = _rw_scan(s_c, r_l, k_l, v_l, lw_l, a_l, kkp, kap, rkp)
            h_lat = _rw_readout(out_l[:2], out_l[2:], gate_l, h_lat, ml, lg, lb, wo, min(T, 256))
            if need_ctx:
                h_ctx = _rw_readout(out_c[:2], out_c[2:], gate_c, h_ctx, mc, lg, lb, wo, min(C, 256))
            o_l = o_c = wo = None
        elif kind == 1:
            w = bf(jnp.concatenate([at_wq[j], at_wk[j], at_wv[j]], axis=1))
            gq, gk, wo = at_gq[j][None, :], at_gk[j][None, :], bf(at_wo[j])
            cos, sin = _rope_tables(T, AT_HEAD)
            q_l, k_l, v_l = _gqa_qkv(h_lat, ml, g_mix, w, gq, gk, cos, sin, tl, True)
            q_c, k_c, v_c = _gqa_qkv(h_ctx, mc, g_mix, w, gq, gk, cos, sin, tc, False)
            k_all = jnp.concatenate([k_l, k_c], axis=1)
            v_all = jnp.concatenate([v_l, v_c], axis=1)
            grp = q_l.shape[2] // k_l.shape[2]
            o_l = _flash(q_l, k_all, v_all, AT_HEAD, grp, AT_KV_HEADS, tl, math.gcd(T + C, 256))
            o_c = _flash(q_c, k_c, v_c, AT_HEAD, grp, 1, tc, tc) if need_ctx else None
        else:
            w, wo = bf(na_wqkv[j]), bf(na_wo[j])
            gq, gk = na_gq[j][None, :], na_gk[j][None, :]
            q_l, k_l, v_l = _na_qkv(h_lat, ml, g_mix, w, gq, gk, min(T, 256))
            q_c, k_c, v_c = _na_qkv(h_ctx, mc, g_mix, w, gq, gk, min(C, 256))
            o_l = _na_attention(q_l, k_l, v_l, k_c, v_c, na_rpb[j])
            o_c = _flash(q_c, k_c, v_c, NA_HEAD, 1, 2, tc, tc) if need_ctx else None
        g_ffn = norm_ffn[i][None, :]
        w1, w3, w2 = bf(ff_w1[i]), bf(ff_w3[i]), bf(ff_w2[i])
        h_lat = _ffn(h_lat, ml, g_ffn, w1, w3, w2, tl_ffn, o_l, wo)
        if need_ctx:
            o_c = None if o_c is None else o_c.reshape(1, B * C, -1)
            h_ctx = _ffn(h_ctx.reshape(1, B * C, D), mc, g_ffn, w1, w3, w2, min(B * C, 1024), o_c, wo).reshape(B, C, D)
    return h_lat
```

```python
import functools
import math

import numpy as np
import jax
import jax.numpy as jnp
from jax import lax
from jax.experimental import pallas as pl
from jax.experimental.pallas import tpu as pltpu

F32 = jnp.float32
BF16 = jnp.bfloat16

GRID_W = 64
RW_HEAD = 64
AT_HEAD = 128
AT_KV_HEADS = 2
NA_HEAD = 64
WIN_H = 8
WIN_W = 16
ROPE_BASE = 10000.0
RMS_EPS = 1e-6
LNX_EPS = 64e-5
NORM_EPS = 1e-12
NEG_BIG = -1e30
EXP_M05 = math.exp(-0.5)
LOG2_E = math.log2(math.e)

SCAN_CHUNK = 64
NA_ROWS_PER_STEP = 2
PAIR_LANES = 128
VMEM_LIMIT_BYTES_V7X = 48 * 1024 * 1024


def _params(*sem):
    return pltpu.CompilerParams(dimension_semantics=sem, vmem_limit_bytes=VMEM_LIMIT_BYTES_V7X)


def _sigmoid(x):
    return 1.0 / (1.0 + jnp.exp(-x))


def _rms(x, eps):
    return x * lax.rsqrt(jnp.mean(x * x, axis=-1, keepdims=True) + eps)


def _norm_mod(x, g, mod_ref, k):
    return (_rms(x, RMS_EPS) * g) * (1.0 + mod_ref[0, k + 1:k + 2, :]) + mod_ref[0, k:k + 1, :]


def _dot(a, b):
    return jnp.dot(a, b, preferred_element_type=F32)


def _dot_nt(a, b):
    return lax.dot_general(a, b, (((1,), (1,)), ((), ())), preferred_element_type=F32)


def _dot_tn(a, b):
    return lax.dot_general(a, b, (((0,), (0,)), ((), ())), preferred_element_type=F32)


def _split2(x):
    hi = x.astype(BF16)
    return hi, (x - hi.astype(F32)).astype(BF16)


def _dot2(x, m01):
    hi, lo = _split2(x)
    return _dot(hi, m01) + _dot(lo, m01)


def _head_indicators(d, n):
    ind = jnp.where(lax.broadcasted_iota(jnp.int32, (d, 128), 0) // n
                    == lax.broadcasted_iota(jnp.int32, (d, 128), 1), 1.0, 0.0).astype(BF16)
    ind_t = jnp.where(lax.broadcasted_iota(jnp.int32, (128, d), 1) // n
                      == lax.broadcasted_iota(jnp.int32, (128, d), 0), 1.0, 0.0).astype(BF16)
    return ind, ind_t


def _mod_index(bm):
    return (lambda b, *_: (b, 0, 0)) if bm > 1 else (lambda b, *_: (0, 0, 0))


def _mod_kernel(x_ref, w_ref, b_ref, o_ref):
    x = x_ref[...]
    s = x * _sigmoid(x)
    w = w_ref[0]
    sh, sm = _split2(s)
    wh, wm = _split2(w)
    o_ref[0] = _dot(sh, wh) + _dot(sh, wm) + _dot(sm, wh) + b_ref[0]


def _modulation(cc, mod_w, mod_b):
    depth, d, e = mod_w.shape
    tn = 1536
    return pl.pallas_call(
        _mod_kernel,
        grid=(depth, e // tn),
        in_specs=[pl.BlockSpec((8, d), lambda l, j: (0, 0)),
                  pl.BlockSpec((1, d, tn), lambda l, j: (l, 0, j)),
                  pl.BlockSpec((1, 1, tn), lambda l, j: (l, 0, j))],
        out_specs=pl.BlockSpec((1, 8, tn), lambda l, j: (l, 0, j)),
        out_shape=jax.ShapeDtypeStruct((depth, 8, e), F32),
        compiler_params=_params("parallel", "parallel"),
        name="modulation",
    )(cc, mod_w, mod_b.reshape(depth, 1, e))


def _ffn_kernel(*refs, mixer_proj):
    if mixer_proj:
        attn_ref, wo_ref, x_ref, mod_ref, g_ref, w1_ref, w3_ref, w2_ref, o_ref, f_scr, acc_scr, h_scr = refs
    else:
        x_ref, mod_ref, g_ref, w1_ref, w3_ref, w2_ref, o_ref, f_scr, acc_scr = refs
    j = pl.program_id(2)

    @pl.when(j == 0)
    def _():
        h = x_ref[0]
        if mixer_proj:
            h = h + mod_ref[0, 2:3, :] * _dot(attn_ref[0], wo_ref[...])
            h_scr[...] = h
        f_scr[...] = _norm_mod(h, g_ref[...], mod_ref, 3).astype(BF16)
        acc_scr[...] = jnp.zeros_like(acc_scr)

    f = f_scr[...]
    h1 = _dot(f, w1_ref[...])
    h3 = _dot(f, w3_ref[...])
    hm = (h1 * _sigmoid(h1)) * h3
    acc_scr[...] += _dot(hm.astype(BF16), w2_ref[...])

    @pl.when(j == pl.num_programs(2) - 1)
    def _():
        h = h_scr[...] if mixer_proj else x_ref[0]
        o_ref[0] = h + mod_ref[0, 5:6, :] * acc_scr[...]


def _ffn(x, mod, g, w1, w3, w2, tm, attn=None, wo=None):
    b, n, d = x.shape
    f = w1.shape[1]
    tf = 256
    tile = pl.BlockSpec((1, tm, d), lambda b_, i, j: (b_, i, 0))
    in_specs = [tile,
                pl.BlockSpec((1, 6, d), _mod_index(mod.shape[0])),
                pl.BlockSpec((1, d), lambda b_, i, j: (0, 0)),
                pl.BlockSpec((d, tf), lambda b_, i, j: (0, j)),
                pl.BlockSpec((d, tf), lambda b_, i, j: (0, j)),
                pl.BlockSpec((tf, d), lambda b_, i, j: (j, 0))]
    scratch = [pltpu.VMEM((tm, d), BF16), pltpu.VMEM((tm, d), F32)]
    args = (x, mod, g, w1, w3, w2)
    if attn is not None:
        k = attn.shape[2]
        in_specs = [pl.BlockSpec((1, tm, k), lambda b_, i, j: (b_, i, 0)),
                    pl.BlockSpec((k, d), lambda b_, i, j: (0, 0))] + in_specs
        scratch = scratch + [pltpu.VMEM((tm, d), F32)]
        args = (attn, wo) + args
    return pl.pallas_call(
        functools.partial(_ffn_kernel, mixer_proj=attn is not None),
        grid=(b, n // tm, f // tf),
        in_specs=in_specs,
        out_specs=tile,
        out_shape=jax.ShapeDtypeStruct((b, n, d), F32),
        scratch_shapes=scratch,
        compiler_params=_params("parallel", "parallel", "arbitrary"),
        name="ffn",
    )(*args)


def _gqa_qkv_kernel(x_ref, mod_ref, g_ref, w_ref, gq_ref, gk_ref, gqs_ref, gks_ref, cos_ref, sin_ref,
                    q_ref, k_ref, v_ref, *, rope, n_q, n_kv, hd, scale):
    a = _norm_mod(x_ref[0], g_ref[...], mod_ref, 0).astype(BF16)
    qkv = _dot(a, w_ref[...])
    n_qk = n_q + n_kv
    base_sw = (n_q + 2 * n_kv) * hd

    def head(h, g, g_sw):
        xh = qkv[:, h * hd:(h + 1) * hd]
        rs = lax.rsqrt(jnp.mean(xh * xh, axis=-1, keepdims=True) + RMS_EPS)
        y = xh * rs * g
        if rope:
            partner = qkv[:, base_sw + h * hd:base_sw + (h + 1) * hd] * rs * g_sw
            y = y * cos_ref[...] + partner * sin_ref[...]
        return y

    for h in range(n_q):
        q_ref[0, :, h * hd:(h + 1) * hd] = (head(h, gq_ref[...], gqs_ref[...]) * scale).astype(BF16)
    for h in range(n_kv):
        k_ref[0, :, h * hd:(h + 1) * hd] = head(n_q + h, gk_ref[...], gks_ref[...]).astype(BF16)
    v_ref[0] = qkv[:, n_qk * hd:(n_qk + n_kv) * hd].astype(BF16)


def _swap_pairs(w):
    return w.reshape(w.shape[:-1] + (-1, 2))[..., ::-1].reshape(w.shape)


def _gqa_qkv(x, mod, g, w, gq, gk, cos, sin, tm, rope):
    b, n, d = x.shape
    hd = AT_HEAD
    n_kv = AT_KV_HEADS
    n_q = w.shape[1] // hd - 2 * n_kv
    if rope:
        w = jnp.concatenate([w, _swap_pairs(w[:, :(n_q + n_kv) * hd])], axis=1)
    kern = functools.partial(_gqa_qkv_kernel, rope=rope, n_q=n_q, n_kv=n_kv, hd=hd, scale=hd ** -0.5 * LOG2_E)
    tab = (lambda b_, i: (i, 0)) if rope else (lambda b_, i: (0, 0))
    vec = pl.BlockSpec((1, hd), lambda b_, i: (0, 0))
    return pl.pallas_call(
        kern,
        grid=(b, n // tm),
        in_specs=[pl.BlockSpec((1, tm, d), lambda b_, i: (b_, i, 0)),
                  pl.BlockSpec((1, 6, d), _mod_index(mod.shape[0])),
                  pl.BlockSpec((1, d), lambda b_, i: (0, 0)),
                  pl.BlockSpec(w.shape, lambda b_, i: (0, 0)),
                  vec, vec, vec, vec,
                  pl.BlockSpec((tm, hd), tab),
                  pl.BlockSpec((tm, hd), tab)],
        out_specs=[pl.BlockSpec((1, tm, n_q * hd), lambda b_, i: (b_, i, 0)),
                   pl.BlockSpec((1, tm, n_kv * hd), lambda b_, i: (b_, i, 0)),
                   pl.BlockSpec((1, tm, n_kv * hd), lambda b_, i: (b_, i, 0))],
        out_shape=[jax.ShapeDtypeStruct((b, n, n_q * hd), BF16),
                   jax.ShapeDtypeStruct((b, n, n_kv * hd), BF16),
                   jax.ShapeDtypeStruct((b, n, n_kv * hd), BF16)],
        compiler_params=_params("parallel", "parallel"),
        name="gqa_qkv",
    )(x, mod, g, w, gq, gk, _swap_pairs(gq), _swap_pairs(gk), cos, sin)


def _flash_kernel(q_ref, k_ref, v_ref, o_ref, m_scr, l_scr, acc_scr, *, kvb, grp, hd):
    j = pl.program_id(3)

    @pl.when(j == 0)
    def _():
        m_scr[...] = jnp.full_like(m_scr, NEG_BIG)
        l_scr[...] = jnp.zeros_like(l_scr)
        acc_scr[...] = jnp.zeros_like(acc_scr)

    heads = [(kh, kh * grp + g) for kh in range(kvb) for g in range(grp)]
    s_l = [_dot_nt(k_ref[0, :, kh * hd:(kh + 1) * hd], q_ref[0, :, hq * hd:(hq + 1) * hd])
           for kh, hq in heads]
    p_l, alpha_l = [], []
    for (kh, hq), s in zip(heads, s_l):
        m_prev = m_scr[hq]
        m_new = jnp.maximum(m_prev, jnp.max(s, axis=0, keepdims=True))
        alpha = jnp.exp2(m_prev - m_new)
        p = jnp.exp2(s - m_new)
        l_scr[hq] = alpha * l_scr[hq] + jnp.sum(p, axis=0, keepdims=True)
        m_scr[hq] = m_new
        p_l.append(p.astype(BF16))
        alpha_l.append(alpha)
    for (kh, hq), p, alpha in zip(heads, p_l, alpha_l):
        pv = _dot_tn(v_ref[0, :, kh * hd:(kh + 1) * hd], p)
        acc_scr[hq] = alpha * acc_scr[hq] + pv

    @pl.when(j == pl.num_programs(3) - 1)
    def _():
        for hq in range(kvb * grp):
            o_ref[0, :, hq * hd:(hq + 1) * hd] = (acc_scr[hq] / l_scr[hq]).T.astype(BF16)


def _flash(q, k, v, hd, grp, kvb, tq, tk):
    b, nq, dq = q.shape
    nk, dk = k.shape[1], k.shape[2]
    nblk = dk // (kvb * hd)
    kern = functools.partial(_flash_kernel, kvb=kvb, grp=grp, hd=hd)
    nh = kvb * grp
    return pl.pallas_call(
        kern,
        grid=(b, nblk, nq // tq, nk // tk),
        in_specs=[pl.BlockSpec((1, tq, nh * hd), lambda b_, g, i, j: (b_, i, g)),
                  pl.BlockSpec((1, tk, kvb * hd), lambda b_, g, i, j: (b_, j, g)),
                  pl.BlockSpec((1, tk, kvb * hd), lambda b_, g, i, j: (b_, j, g))],
        out_specs=pl.BlockSpec((1, tq, nh * hd), lambda b_, g, i, j: (b_, i, g)),
        out_shape=jax.ShapeDtypeStruct((b, nq, dq), BF16),
        scratch_shapes=[pltpu.VMEM((nh, 1, tq), F32), pltpu.VMEM((nh, 1, tq), F32),
                        pltpu.VMEM((nh, hd, tq), F32)],
        compiler_params=_params("parallel", "parallel", "parallel", "arbitrary"),
        name="flash_attention",
    )(q, k, v)


def _na_qkv_kernel(x_ref, mod_ref, g_ref, w_ref, gq_ref, gk_ref, q_ref, k_ref, v_ref, *, nh, hd, scale):
    a = _norm_mod(x_ref[0], g_ref[...], mod_ref, 0).astype(BF16)
    qkv = _dot(a, w_ref[...])
    d = nh * hd
    ind, ind_t = _head_indicators(d, hd)

    def head_rms(x):
        return x * _dot2(lax.rsqrt(_dot2(x * x, ind) * (1.0 / hd) + RMS_EPS), ind_t)

    q_ref[0] = (head_rms(qkv[:, :d]) * (gq_ref[...] * scale)).astype(BF16)
    k_ref[0] = (head_rms(qkv[:, d:2 * d]) * gk_ref[...]).astype(BF16)
    v_ref[0] = qkv[:, 2 * d:].astype(BF16)


def _na_qkv(x, mod, g, w, gq, gk, tm):
    b, n, d = x.shape
    hd = NA_HEAD
    nh = w.shape[1] // (3 * hd)
    kern = functools.partial(_na_qkv_kernel, nh=nh, hd=hd, scale=hd ** -0.5 * LOG2_E)
    spec_o = pl.BlockSpec((1, tm, nh * hd), lambda b_, i: (b_, i, 0))
    return pl.pallas_call(
        kern,
        grid=(b, n // tm),
        in_specs=[pl.BlockSpec((1, tm, d), lambda b_, i: (b_, i, 0)),
                  pl.BlockSpec((1, 6, d), _mod_index(mod.shape[0])),
                  pl.BlockSpec((1, d), lambda b_, i: (0, 0)),
                  pl.BlockSpec(w.shape, lambda b_, i: (0, 0)),
                  pl.BlockSpec((1, nh * hd), lambda b_, i: (0, 0)),
                  pl.BlockSpec((1, nh * hd), lambda b_, i: (0, 0))],
        out_specs=[spec_o, spec_o, spec_o],
        out_shape=[jax.ShapeDtypeStruct((b, n, nh * hd), BF16)] * 3,
        compiler_params=_params("parallel", "parallel"),
        name="na_qkv",
    )(x, mod, g, w, jnp.tile(gq, (1, nh)), jnp.tile(gk, (1, nh)))


def _na_kernel(cls_ref, q_ref, k_ref, v_ref, kc_ref, vc_ref, bias0_ref, bias1_ref, o_ref, *, hb, hd, width, kh, rows):
    del cls_ref
    hp = 128 // hd
    pairs = [slice(p * 128, (p + 1) * 128) for p in range(hb // hp)]
    lane = lax.broadcasted_iota(jnp.int32, (1, 128), 1)
    own = [(lane >= j * hd) & (lane < (j + 1) * hd) for j in range(hp)]
    bias_refs = (bias0_ref, bias1_ref)
    wins, qrows = [], []
    for rr in range(NA_ROWS_PER_STEP):
        r = pl.program_id(2) * NA_ROWS_PER_STEP + rr
        rs = jnp.clip(r - kh // 2, 0, rows - kh)
        wins.append(pl.ds(pl.multiple_of(rs * width, width), kh * width))
        qrows.append(slice(rr * width, (rr + 1) * width))
    heads = [(rr, p, j) for rr in range(NA_ROWS_PER_STEP) for p in range(len(pairs)) for j in range(hp)]
    q_l = [jnp.where(own[j], q_ref[0, qrows[rr], pairs[p]], 0.0).astype(BF16) for rr, p, j in heads]
    s_l = [_dot_nt(q_l[i], k_ref[0, wins[rr], pairs[p]]) for i, (rr, p, j) in enumerate(heads)]
    sc_l = [_dot_nt(q_l[i], kc_ref[0, :, pairs[p]]) for i, (rr, p, j) in enumerate(heads)]
    p_l, pc_l, l_l = [], [], []
    for i, (rr, p, j) in enumerate(heads):
        s = s_l[i] + bias_refs[rr][0, p * hp + j]
        sc = sc_l[i]
        m = jnp.maximum(jnp.max(s, axis=-1, keepdims=True), jnp.max(sc, axis=-1, keepdims=True))
        e = jnp.exp2(s - m)
        ec = jnp.exp2(sc - m)
        l_l.append(jnp.sum(e, axis=-1, keepdims=True) + jnp.sum(ec, axis=-1, keepdims=True))
        p_l.append(e.astype(BF16))
        pc_l.append(ec.astype(BF16))
    o_l = [_dot(p_l[i], v_ref[0, wins[rr], pairs[p]]) + _dot(pc_l[i], vc_ref[0, :, pairs[p]])
           for i, (rr, p, j) in enumerate(heads)]
    for rr in range(NA_ROWS_PER_STEP):
        for p in range(len(pairs)):
            base = (rr * len(pairs) + p) * hp
            o = o_l[base] / l_l[base]
            for j in range(1, hp):
                o = jnp.where(own[j], o_l[base + j] / l_l[base + j], o)
            o_ref[0, qrows[rr], pairs[p]] = o.astype(BF16)


def _na_tables(rpb, rows):
    kh, kw = min(WIN_H, rows), WIN_W
    r_idx = np.arange(rows)
    delta = np.clip(r_idx - kh // 2, 0, rows - kh) - r_idx
    classes, cls_of_row = np.unique(delta, return_inverse=True)
    c_idx = np.arange(GRID_W)
    cs = np.clip(c_idx - kw // 2, 0, GRID_W - kw)
    kcol = np.arange(GRID_W)
    valid = (kcol[None, :] >= cs[:, None]) & (kcol[None, :] < cs[:, None] + kw)
    pad = GRID_W
    rp = jnp.pad(rpb.astype(F32), ((0, 0), (0, 0), (pad, pad)))
    toep = jnp.stack([rp[:, :, pad + WIN_W - 1 - c:pad + WIN_W - 1 - c + GRID_W] for c in range(GRID_W)], axis=2)
    toep = jnp.where(valid[None, None, :, :], toep * LOG2_E, NEG_BIG)
    tab = jnp.stack([toep[:, int(dl) + WIN_H - 1:int(dl) + WIN_H - 1 + kh] for dl in classes], axis=0)
    tab = tab.transpose(0, 1, 3, 2, 4).reshape(len(classes), rpb.shape[0], GRID_W, kh * GRID_W)
    return tab, jnp.asarray(cls_of_row, jnp.int32), kh


def _na_attention(q, k, v, kc, vc, rpb):
    b, t, d = q.shape
    c = kc.shape[1]
    hd = NA_HEAD
    hb = d // hd
    rows = t // GRID_W
    tab, cls_of_row, kh = _na_tables(rpb, rows)
    kern = functools.partial(_na_kernel, hb=hb, hd=hd, width=GRID_W, kh=kh, rows=rows)
    grid_spec = pltpu.PrefetchScalarGridSpec(
        num_scalar_prefetch=1,
        grid=(b, d // (hb * hd), rows // NA_ROWS_PER_STEP),
        in_specs=[pl.BlockSpec((1, NA_ROWS_PER_STEP * GRID_W, hb * hd), lambda b_, g, r, cls: (b_, r, g)),
                  pl.BlockSpec((1, t, hb * hd), lambda b_, g, r, cls: (b_, 0, g), pipeline_mode=pl.Buffered(1)),
                  pl.BlockSpec((1, t, hb * hd), lambda b_, g, r, cls: (b_, 0, g), pipeline_mode=pl.Buffered(1)),
                  pl.BlockSpec((1, c, hb * hd), lambda b_, g, r, cls: (b_, 0, g)),
                  pl.BlockSpec((1, c, hb * hd), lambda b_, g, r, cls: (b_, 0, g)),
                  pl.BlockSpec((1, hb, GRID_W, kh * GRID_W),
                               lambda b_, g, r, cls: (cls[NA_ROWS_PER_STEP * r], g, 0, 0)),
                  pl.BlockSpec((1, hb, GRID_W, kh * GRID_W),
                               lambda b_, g, r, cls: (cls[NA_ROWS_PER_STEP * r + 1], g, 0, 0))],
        out_specs=pl.BlockSpec((1, NA_ROWS_PER_STEP * GRID_W, hb * hd), lambda b_, g, r, cls: (b_, r, g)),
    )
    return pl.pallas_call(
        kern,
        grid_spec=grid_spec,
        out_shape=jax.ShapeDtypeStruct((b, t, d), BF16),
        compiler_params=_params("parallel", "parallel", "arbitrary"),
        name="na_attention",
    )(cls_of_row, q, k, v, kc, vc, tab, tab)


def _rw_feat_kernel(x_ref, xp_ref, xn_ref, mod_ref, g_ref, mu_ref, wr_ref, wk_ref, wv_ref,
                    w1_ref, w2_ref, a1_ref, a2_ref, g1_ref, g2_ref, w0_ref, a0_ref,
                    r_ref, k_ref, v_ref, lw_ref, a_ref, gate_ref, *, tm, lora_w, lora_a):
    i = pl.program_id(1)
    g = g_ref[...]
    a = _norm_mod(x_ref[0], g, mod_ref, 0)
    a_prev = _norm_mod(xp_ref[0], g, mod_ref, 0)[7:8, :]
    a_next = _norm_mod(xn_ref[0], g, mod_ref, 0)[0:1, :]
    a_prev = jnp.where(i == 0, 0.0, a_prev)
    a_next = jnp.where(i == pl.num_programs(1) - 1, 0.0, a_next)
    row = lax.broadcasted_iota(jnp.int32, (tm, 1), 0)
    prev = jnp.where(row == 0, a_prev, pltpu.roll(a, 1, 0))
    nxt = jnp.where(row == tm - 1, a_next, pltpu.roll(a, tm - 1, 0))
    xx = 0.5 * (prev + nxt) - a

    a_b = a.astype(BF16)
    xx_b = xx.astype(BF16)

    def mix(j):
        return a_b + xx_b * mu_ref[j:j + 1, :].astype(BF16)

    r_ref[0] = _dot(mix(0), wr_ref[...])
    k_ref[0] = _dot(mix(2), wk_ref[...])
    v_ref[0] = _dot(mix(3), wv_ref[...])
    tw = jnp.tanh(_dot(mix(1), w1_ref[...]))
    ta = _dot(mix(4), a1_ref[...])
    for e in range(2):
        lora = _dot(tw[:, e * lora_w:(e + 1) * lora_w].astype(BF16), w2_ref[e])
        lw_ref[e, 0] = -EXP_M05 * _sigmoid(w0_ref[e:e + 1, :] + lora)
        la = _dot(ta[:, e * lora_a:(e + 1) * lora_a].astype(BF16), a2_ref[e])
        a_ref[e, 0] = _sigmoid(a0_ref[e:e + 1, :] + la)
    gate_ref[0] = _dot(_sigmoid(_dot(mix(5), g1_ref[...])).astype(BF16), g2_ref[...]).astype(BF16)


def _rw_features(x, mod, g, p, tm):
    b, n, d = x.shape
    nb8 = n // 8
    tb = tm // 8
    lora_w = p["w2"].shape[1]
    lora_a = p["a2"].shape[1]
    kern = functools.partial(_rw_feat_kernel, tm=tm, lora_w=lora_w, lora_a=lora_a)
    full = lambda arr: pl.BlockSpec(arr.shape, lambda b_, i: (0,) * arr.ndim)
    tile = pl.BlockSpec((1, tm, d), lambda b_, i: (b_, i, 0))
    tile2 = pl.BlockSpec((2, 1, tm, d), lambda b_, i: (0, b_, i, 0))
    names = ("mu", "wr", "wk", "wv", "w1", "w2", "a1", "a2", "g1", "g2", "w0", "a0")
    return pl.pallas_call(
        kern,
        grid=(b, n // tm),
        in_specs=[tile,
                  pl.BlockSpec((1, 8, d), lambda b_, i: (b_, jnp.maximum(i * tb - 1, 0), 0)),
                  pl.BlockSpec((1, 8, d), lambda b_, i: (b_, jnp.minimum((i + 1) * tb, nb8 - 1), 0)),
                  pl.BlockSpec((1, 6, d), _mod_index(mod.shape[0])),
                  pl.BlockSpec((1, d), lambda b_, i: (0, 0))] + [full(p[nm]) for nm in names],
        out_specs=[tile, tile, tile, tile2, tile2, tile],
        out_shape=[jax.ShapeDtypeStruct((b, n, d), F32)] * 3
        + [jax.ShapeDtypeStruct((2, b, n, d), F32)] * 2 + [jax.ShapeDtypeStruct((b, n, d), BF16)],
        compiler_params=_params("parallel", "parallel"),
        name="rwkv_features",
    )(x, x, x, mod, g, *[p[nm] for nm in names])


def _rw_scan_kernel(s0_ref, rf_ref, kf_ref, vf_ref, rr_ref, kr_ref, vr_ref, lwf_ref, lwr_ref, af_ref, ar_ref,
                    kkp_ref, kap_ref, rkp_ref, yf_ref, yr_ref, bonf_ref, bonr_ref, sf_ref, s_scr, *, L, H, N):
    c = pl.program_id(1)

    @pl.when(c == 0)
    def _():
        s_scr[...] = s0_ref[:, 0]

    assert L == N and 2 * N == PAIR_LANES
    P = PAIR_LANES
    D = H * N
    n_pairs = D // P
    lane = lax.broadcasted_iota(jnp.int32, (1, P), 1)
    own0 = lane < N
    blk = ((lax.broadcasted_iota(jnp.int32, (P, P), 0) < N) == (lax.broadcasted_iota(jnp.int32, (P, P), 1) < N))

    def bd(x):
        return jnp.where(blk, jnp.concatenate([x, x], axis=0), 0.0).astype(x.dtype)

    ri = lax.broadcasted_iota(jnp.int32, (L, L), 0)
    ci = lax.broadcasted_iota(jnp.int32, (L, L), 1)
    eye2 = jnp.where(lax.broadcasted_iota(jnp.int32, (L, P), 0)
                     == (lax.broadcasted_iota(jnp.int32, (L, P), 1) & (L - 1)), 1.0, 0.0).astype(F32)
    rg = lax.broadcasted_iota(jnp.int32, (2 * L, 4 * L), 0)
    cg = lax.broadcasted_iota(jnp.int32, (2 * L, 4 * L), 1) & (L - 1)
    rgl = rg & (L - 1)
    strict_rows = jnp.where(rg < L, 1, 0)
    ind, _ = _head_indicators(D, N)
    n_fact = int(math.log2(L))
    pairs = [slice(p * P, (p + 1) * P) for p in range(n_pairs)]

    dirs = []
    for rev, (r_ref, k_ref, v_ref, lw_ref, a_ref) in enumerate(
            ((rf_ref, kf_ref, vf_ref, lwf_ref, af_ref), (rr_ref, kr_ref, vr_ref, lwr_ref, ar_ref))):
        dist = (ci - ri) if rev else (ri - ci)
        incl_b = jnp.where(dist >= 0, 1.0, 0.0).astype(BF16)
        mask_g = ((cg - rgl) if rev else (rgl - cg)) >= strict_rows
        lw = lw_ref[0, 0]
        l1, l2 = _split2(lw)
        cw = _dot(incl_b, l1) + _dot(incl_b, l2)
        tot = cw[0:1, :] if rev else cw[L - 1:L, :]
        a = a_ref[0, 0]
        k = k_ref[0]
        r = r_ref[0]
        v = v_ref[0]
        kk_raw = k * kkp_ref[...]
        kd = k * (1.0 + (a - 1.0) * kap_ref[...])
        hs = _dot2(jnp.concatenate([kk_raw * kk_raw, r * kd * rkp_ref[...]], axis=0), ind)
        dirs.append(dict(
            mask_g=mask_g, a=a, v=v, kk_raw=kk_raw,
            inv_norm=jnp.minimum(lax.rsqrt(hs[:L]), 1.0 / NORM_EPS), rk_sum=hs[L:],
            e_prev=jnp.exp(cw - lw), e_neg=jnp.exp(-cw), e_rem=jnp.exp(tot - cw), e_tot=jnp.exp(tot),
            r_t=r * jnp.exp(cw), k_t=kd * jnp.exp(-cw), k_h=kd * jnp.exp(tot - cw)))

    items = [(e, p) for e in range(2) for p in range(n_pairs)]
    idx = range(len(items))
    x_ar, z_bd, bk, vb = [], [], [], []
    for e, p in items:
        d, sl = dirs[e], pairs[p]
        h0 = p * (P // N)
        spread = lambda t: jnp.where(own0, t[:, h0:h0 + 1], t[:, h0 + 1:h0 + 2])
        kk = d["kk_raw"][:, sl] * spread(d["inv_norm"])
        b = kk * d["a"][:, sl]
        v_p = d["v"][:, sl]
        bon_ref = bonr_ref if e else bonf_ref
        bon_ref[0, :, sl] = (spread(d["rk_sum"]) * v_p).astype(BF16)
        x_ar.append(jnp.concatenate([-kk * d["e_prev"][:, sl], d["r_t"][:, sl]], axis=0).astype(BF16))
        z_b = (b * d["e_neg"][:, sl]).astype(BF16)
        z_k = d["k_t"][:, sl].astype(BF16)
        zero = jnp.zeros_like(z_b)
        z_bd.append(jnp.concatenate([jnp.where(own0, z_b, zero), jnp.where(own0, zero, z_b),
                                     jnp.where(own0, z_k, zero), jnp.where(own0, zero, z_k)], axis=0))
        bk.append(jnp.concatenate([b * d["e_rem"][:, sl], d["k_h"][:, sl]], axis=0).astype(BF16))
        vb.append(v_p.astype(BF16))
    gram = [jnp.where(dirs[e]["mask_g"], _dot_nt(x_ar[i], z_bd[i]), 0.0) for i, (e, p) in enumerate(items)]
    s_old = [s_scr[e, p] for e, p in items]
    xs = [_dot_nt(x_ar[i], s_old[i].astype(BF16)) for i in idx]
    vbd = [bd(vb[i]) for i in idx]
    rhs = [xs[i][:L] + _dot(gram[i][:L, P:].astype(BF16), vbd[i]) for i in idx]

    qb = [gram[i][:L, :P].astype(BF16) for i in idx]
    t_inv = [eye2 + gram[i][:L, :P] for i in idx]
    q = [_dot(qb[i], bd(qb[i])) for i in idx]
    for _ in range(n_fact - 2):
        qb = [q[i].astype(BF16) for i in idx]
        st = [_dot(jnp.concatenate([t_inv[i].astype(BF16), qb[i]], axis=0), bd(qb[i])) for i in idx]
        t_inv = [t_inv[i] + st[i][:L] for i in idx]
        q = [st[i][L:] for i in idx]
    st = [_dot(t_inv[i].astype(BF16), bd(q[i].astype(BF16))) for i in idx]
    t_inv = [t_inv[i] + st[i] for i in idx]

    u = [_dot(t_inv[i].astype(BF16), bd(rhs[i].astype(BF16))).astype(BF16) for i in idx]
    y = [xs[i][L:] + _dot(gram[i][L:, :].astype(BF16), jnp.concatenate([bd(u[i]), vbd[i]], axis=0)) for i in idx]
    uv = [jnp.concatenate([u[i], vb[i]], axis=0) for i in idx]
    s_upd = [_dot_tn(uv[i], bk[i]) for i in idx]
    for i, (e, p) in enumerate(items):
        y_ref = yr_ref if e else yf_ref
        y_ref[0, :, pairs[p]] = y[i].astype(BF16)
        s_scr[e, p] = jnp.where(blk, s_old[i] * dirs[e]["e_tot"][:, pairs[p]] + s_upd[i], 0.0)

    @pl.when(c == pl.num_programs(1) - 1)
    def _():
        sf_ref[:, 0] = s_scr[...]


def _rw_scan(s0, r, k, v, lw, a, kkp, kap, rkp):
    b, n, d = r.shape
    L = SCAN_CHUNK
    N = RW_HEAD
    H = d // N
    nc = n // L
    kern = functools.partial(_rw_scan_kernel, L=L, H=H, N=N)
    tok_f = pl.BlockSpec((1, L, d), lambda b_, c: (b_, c, 0))
    tok_r = pl.BlockSpec((1, L, d), lambda b_, c: (b_, nc - 1 - c, 0))
    dir_f = pl.BlockSpec((1, 1, L, d), lambda b_, c: (0, b_, c, 0))
    dir_r = pl.BlockSpec((1, 1, L, d), lambda b_, c: (1, b_, nc - 1 - c, 0))
    vec = pl.BlockSpec((1, d), lambda b_, c: (0, 0))
    n_pairs = d // PAIR_LANES
    st = pl.BlockSpec((2, 1, n_pairs, PAIR_LANES, PAIR_LANES), lambda b_, c: (0, b_, 0, 0, 0))
    tok_shape = jax.ShapeDtypeStruct((b, n, d), BF16)
    return pl.pallas_call(
        kern,
        grid=(b, nc),
        in_specs=[st, tok_f, tok_f, tok_f, tok_r, tok_r, tok_r, dir_f, dir_r, dir_f, dir_r, vec, vec, vec],
        out_specs=[tok_f, tok_r, tok_f, tok_r, st],
        out_shape=[tok_shape, tok_shape, tok_shape, tok_shape, jax.ShapeDtypeStruct((2, b, n_pairs, PAIR_LANES, PAIR_LANES), F32)],
        scratch_shapes=[pltpu.VMEM((2, n_pairs, PAIR_LANES, PAIR_LANES), F32)],
        compiler_params=_params("parallel", "arbitrary"),
        name="rwkv_scan",
    )(s0, r, k, v, r, k, v, lw, lw, a, a, kkp, kap, rkp)


def _rw_out_kernel(yf_ref, yr_ref, bonf_ref, bonr_ref, gate_ref, res_ref, mod_ref, lg_ref, lb_ref, wo_ref, o_ref,
                   *, H, N):
    y = yf_ref[0].astype(F32) + yr_ref[0].astype(F32)
    ind, ind_t = _head_indicators(H * N, N)
    dlt = y - _dot2(_dot2(y, ind) * (1.0 / N), ind_t)
    var = _dot2(dlt * dlt, ind) * (1.0 / N)
    yn = dlt * _dot2(lax.rsqrt(var + LNX_EPS), ind_t)
    bonus = bonf_ref[0].astype(F32) + bonr_ref[0].astype(F32)
    z = (yn * lg_ref[...] + lb_ref[...] + bonus) * gate_ref[0].astype(F32)
    o_ref[0] = res_ref[0] + mod_ref[0, 2:3, :] * _dot(z.astype(BF16), wo_ref[...])


def _rw_readout(ys, bons, gate, res, mod, lg, lb, wo, tm):
    b, n, d = res.shape
    N = RW_HEAD
    kern = functools.partial(_rw_out_kernel, H=d // N, N=N)
    tile = pl.BlockSpec((1, tm, d), lambda b_, i: (b_, i, 0))
    vec = pl.BlockSpec((1, d), lambda b_, i: (0, 0))
    return pl.pallas_call(
        kern,
        grid=(b, n // tm),
        in_specs=[tile, tile, tile, tile, tile, tile, pl.BlockSpec((1, 6, d), _mod_index(mod.shape[0])),
                  vec, vec, pl.BlockSpec((d, d), lambda b_, i: (0, 0))],
        out_specs=tile,
        out_shape=jax.ShapeDtypeStruct((b, n, d), F32),
        compiler_params=_params("parallel", "parallel"),
        name="rwkv_readout",
    )(*ys, *bons, gate, res, mod, lg, lb, wo)


def _rope_tables(n_tok, hd):
    t = jnp.arange(n_tok)
    rows = (t // GRID_W).astype(F32)
    cols = (t % GRID_W).astype(F32)
    d_axis = hd // 2
    inv = jnp.float32(ROPE_BASE) ** (-jnp.arange(0, d_axis, 2, dtype=F32) / d_axis)
    ang = jnp.concatenate([rows[:, None] * inv, cols[:, None] * inv], axis=-1)
    cos = jnp.repeat(jnp.cos(ang), 2, axis=-1)
    sign = jnp.tile(jnp.asarray([-1.0, 1.0], F32), hd // 2)
    sin = jnp.repeat(jnp.sin(ang), 2, axis=-1) * sign
    return cos, sin


def _tiles(n):
    return min(n, 512), min(n, 1024)


def kernel(x, c, ctx, c_ctx, mod_w, mod_b, norm_mix, norm_ffn, ff_w1, ff_w3, ff_w2, rw_mu, rw_wr, rw_wk, rw_wv, rw_wo, rw_w0, rw_w1, rw_w2, rw_a0, rw_a1, rw_a2, rw_g1, rw_g2, rw_kk, rw_ka, rw_rk, rw_lnx_g, rw_lnx_b, at_wq, at_wk, at_wv, at_wo, at_gq, at_gk, na_wqkv, na_wo, na_gq, na_gk, na_rpb):
    B, T, D = x.shape
    C = ctx.shape[1]
    depth = mod_w.shape[0]
    bf = lambda w: w.astype(BF16)

    cc = jnp.concatenate([c, c_ctx[None, :], jnp.zeros((8 - B - 1, D), F32)], axis=0)
    mods = _modulation(cc, mod_w, mod_b)
    mod_lat = mods[:, :B].reshape(depth, B, 6, D)
    mod_ctx = mods[:, B:B + 1].reshape(depth, 1, 6, D)

    tl, tl_ffn = _tiles(T)
    tc, _ = _tiles(C)
    h_lat, h_ctx = x, ctx
    for i in range(depth):
        need_ctx = i < depth - 1
        kind, j = i % 3, i // 3
        ml, mc = mod_lat[i], mod_ctx[i]
        g_mix = norm_mix[i][None, :]
        if kind == 0:
            cat = lambda w: jnp.concatenate([w[0], w[1]], axis=1)
            p = dict(mu=rw_mu[j], wr=bf(rw_wr[j]), wk=bf(rw_wk[j]), wv=bf(rw_wv[j]),
                     w1=bf(cat(rw_w1[j])), w2=bf(rw_w2[j]), a1=bf(cat(rw_a1[j])), a2=bf(rw_a2[j]),
                     g1=bf(rw_g1[j]), g2=bf(rw_g2[j]), w0=rw_w0[j], a0=rw_a0[j])
            kkp, kap, rkp = rw_kk[j][None, :], rw_ka[j][None, :], rw_rk[j].reshape(1, D)
            lg, lb, wo = rw_lnx_g[j][None, :], rw_lnx_b[j][None, :], bf(rw_wo[j])
            r_c, k_c, v_c, lw_c, a_c, gate_c = _rw_features(h_ctx, mc, g_mix, p, min(C, 256))
            r_l, k_l, v_l, lw_l, a_l, gate_l = _rw_features(h_lat, ml, g_mix, p, min(T, 256))
            s0 = jnp.zeros((2, B, D // PAIR_LANES, PAIR_LANES, PAIR_LANES), F32)
            *out_c, s_c = _rw_scan(s0, r_c, k_c, v_c, lw_c, a_c, kkp, kap, rkp)
            *out_l, _ = _rw_scan(s_c, r_l, k_l, v_l, lw_l, a_l, kkp, kap, rkp)
            h_lat = _rw_readout(out_l[:2], out_l[2:], gate_l, h_lat, ml, lg, lb, wo, min(T, 256))
            if need_ctx:
                h_ctx = _rw_readout(out_c[:2], out_c[2:], gate_c, h_ctx, mc, lg, lb, wo, min(C, 256))
            o_l = o_c = wo = None
        elif kind == 1:
            w = bf(jnp.concatenate([at_wq[j], at_wk[j], at_wv[j]], axis=1))
            gq, gk, wo = at_gq[j][None, :], at_gk[j][None, :], bf(at_wo[j])
            cos, sin = _rope_tables(T, AT_HEAD)
            q_l, k_l, v_l = _gqa_qkv(h_lat, ml, g_mix, w, gq, gk, cos, sin, tl, True)
            q_c, k_c, v_c = _gqa_qkv(h_ctx, mc, g_mix, w, gq, gk, cos, sin, tc, False)
            k_all = jnp.concatenate([k_l, k_c], axis=1)
            v_all = jnp.concatenate([v_l, v_c], axis=1)
            grp = q_l.shape[2] // k_l.shape[2]
            o_l = _flash(q_l, k_all, v_all, AT_HEAD, grp, AT_KV_HEADS, tl, math.gcd(T + C, 256))
            o_c = _flash(q_c, k_c, v_c, AT_HEAD, grp, 1, tc, tc) if need_ctx else None
        else:
            w, wo = bf(na_wqkv[j]), bf(na_wo[j])
            gq, gk = na_gq[j][None, :], na_gk[j][None, :]
            q_l, k_l, v_l = _na_qkv(h_lat, ml, g_mix, w, gq, gk, min(T, 256))
            q_c, k_c, v_c = _na_qkv(h_ctx, mc, g_mix, w, gq, gk, min(C, 256))
            o_l = _na_attention(q_l, k_l, v_l, k_c, v_c, na_rpb[j])
            o_c = _flash(q_c, k_c, v_c, NA_HEAD, 1, 2, tc, tc) if need_ctx else None
        g_ffn = norm_ffn[i][None, :]
        w1, w3, w2 = bf(ff_w1[i]), bf(ff_w3[i]), bf(ff_w2[i])
        h_lat = _ffn(h_lat, ml, g_ffn, w1, w3, w2, tl_ffn, o_l, wo)
        if need_ctx:
            o_c = None if o_c is None else o_c.reshape(1, B * C, -1)
            h_ctx = _ffn(h_ctx.reshape(1, B * C, D), mc, g_ffn, w1, w3, w2, min(B * C, 1024), o_c, wo).reshape(B, C, D)
    return h_lat
```
